```python
import jax, jax.numpy as jnp
from jax import lax
import numpy as np

D_MODEL = 4096
BATCH = 4
SEQ = 2048
DEPTH = 2
DEC_BATCH = 32
DEC_SEQ = 4
PAST_LEN = 16384
PAGE_SIZE = 128

D_A = D_MODEL // 2
SHORT_CONV_W = 3
D_B = D_MODEL // 2
POOL_WINDOWS = (2, 4, 8, 16)
N_POOL_GROUPS = len(POOL_WINDOWS)
POOL_GROUP = D_B // N_POOL_GROUPS
POOL_CTX = max(POOL_WINDOWS) - 1
HEAD_DIM = 64
N_HEADS = D_MODEL // HEAD_DIM
N_KV_HEADS = 8
GQA_GROUP = N_HEADS // N_KV_HEADS
QKV_WIDTH = (N_HEADS + 2 * N_KV_HEADS) * HEAD_DIM
WINDOW = 128
ATTN_BLOCK = 128
ROPE_THETA = 10000.0
D_FF = 11008
FFN_CONV_W = 3
N_EVEN = (DEPTH + 1) // 2
N_ODD = DEPTH // 2
EPS = 1e-6

kernel_name = 'hybrid_shortconv_pool_swa_convffn_step'


def rms_norm(x, g):
    xf = x.astype(jnp.float32)
    y = xf * lax.rsqrt(jnp.mean(xf * xf, axis=-1, keepdims=True) + EPS)
    return (y * g.astype(jnp.float32)).astype(x.dtype)


def causal_dwconv(ctx, x, w):
    width = w.shape[0]
    T = x.shape[1]
    z = jnp.concatenate([ctx.astype(x.dtype), x], axis=1)
    y = z[:, 0:T] * w[0]
    for k in range(1, width):
        y = y + z[:, k:k + T] * w[k]
    return y, z[:, z.shape[1] - (width - 1):]


def rope(x, pos):
    half = HEAD_DIM // 2
    inv = ROPE_THETA ** (-jnp.arange(half, dtype=jnp.float32) / half)
    ang = pos.astype(jnp.float32)[:, None] * inv[None, :]
    cos = jnp.cos(ang)[None, :, None, :]
    sin = jnp.sin(ang)[None, :, None, :]
    xf = x.astype(jnp.float32)
    x1, x2 = xf[..., :half], xf[..., half:]
    return jnp.concatenate([x1 * cos - x2 * sin, x2 * cos + x1 * sin], axis=-1).astype(x.dtype)


def pool_mixer(p_in, ctx, pos, w_grp, scale):
    T = p_in.shape[1]
    z = jnp.concatenate([ctx.astype(p_in.dtype), p_in], axis=1)
    zf = z.astype(jnp.float32)
    cs = jnp.concatenate([jnp.zeros_like(zf[:, :1]), lax.cumsum(zf, axis=1)], axis=1)
    xf = p_in.astype(jnp.float32)
    groups = []
    for g, w in enumerate(POOL_WINDOWS):
        sl = slice(g * POOL_GROUP, (g + 1) * POOL_GROUP)
        hi = cs[:, POOL_CTX + 1:POOL_CTX + 1 + T, sl]
        lo = cs[:, POOL_CTX + 1 - w:POOL_CTX + 1 - w + T, sl]
        cnt = jnp.minimum(pos + 1, w).astype(jnp.float32)[None, :, None]
        groups.append((hi - lo) / cnt - xf[..., sl])
    pooled = jnp.stack(groups, axis=2)
    y = jnp.einsum('btgc,gcd->btgd', pooled, w_grp.astype(jnp.float32)).reshape(p_in.shape)
    y = y * scale.astype(jnp.float32)
    return y.astype(p_in.dtype), z[:, z.shape[1] - POOL_CTX:]


def sink_attention(q, k, v, q_pos, k_pos, sinks):
    B, N, Tq = q.shape[:3]
    qg = q.reshape(B, N, Tq, N_KV_HEADS, GQA_GROUP, HEAD_DIM).astype(jnp.float32)
    s = jnp.einsum('bnqkgd,bnskd->bnkgqs', qg, k.astype(jnp.float32)) * (HEAD_DIM ** -0.5)
    kp = k_pos[:, None, :]
    qp = q_pos[:, :, None]
    mask = (kp <= qp) & (kp > qp - WINDOW) & (kp >= 0)
    s = jnp.where(mask[None, :, None, None], s, -jnp.inf)
    sk = sinks.astype(jnp.float32).reshape(N_KV_HEADS, GQA_GROUP)[None, None, :, :, None, None]
    m = jnp.maximum(jnp.max(s, axis=-1, keepdims=True), sk)
    p = jnp.exp(s - m)
    p = p / (jnp.sum(p, axis=-1, keepdims=True) + jnp.exp(sk - m))
    o = jnp.einsum('bnkgqs,bnskd->bnqkgd', p, v.astype(jnp.float32))
    return o.reshape(B, N, Tq, N_HEADS * HEAD_DIM).astype(v.dtype)


def swa_mixer(h, pos, w_qkv, b_qkv, sinks, w_o, k_buf, v_buf, banded):
    B, T, _ = h.shape
    qkv = h @ w_qkv + b_qkv
    nq = N_HEADS * HEAD_DIM
    nk = N_KV_HEADS * HEAD_DIM
    q = rope(qkv[..., :nq].reshape(B, T, N_HEADS, HEAD_DIM), pos)
    k = rope(qkv[..., nq:nq + nk].reshape(B, T, N_KV_HEADS, HEAD_DIM), pos)
    v = qkv[..., nq + nk:].reshape(B, T, N_KV_HEADS, HEAD_DIM)
    if banded:
        nb = T // ATTN_BLOCK
        qb = q.reshape(B, nb, ATTN_BLOCK, N_HEADS, HEAD_DIM)
        kb = k.reshape(B, nb, ATTN_BLOCK, N_KV_HEADS, HEAD_DIM)
        vb = v.reshape(B, nb, ATTN_BLOCK, N_KV_HEADS, HEAD_DIM)
        kk = jnp.concatenate([jnp.concatenate([jnp.zeros_like(kb[:, :1]), kb[:, :-1]], axis=1), kb], axis=2)
        vv = jnp.concatenate([jnp.concatenate([jnp.zeros_like(vb[:, :1]), vb[:, :-1]], axis=1), vb], axis=2)
        blk_pos = pos.reshape(nb, ATTN_BLOCK)
        q_pos = blk_pos
        k_pos = blk_pos[:, :1] - ATTN_BLOCK + jnp.arange(2 * ATTN_BLOCK, dtype=jnp.int32)[None, :]
        o = sink_attention(qb, kk, vv, q_pos, k_pos, sinks).reshape(B, T, nq)
        new_k = k[:, T - WINDOW:]
        new_v = v[:, T - WINDOW:]
    else:
        kk = jnp.concatenate([k_buf.astype(k.dtype), k], axis=1)
        vv = jnp.concatenate([v_buf.astype(v.dtype), v], axis=1)
        q_pos = pos[None, :]
        k_pos = jnp.concatenate([pos[0] - WINDOW + jnp.arange(WINDOW, dtype=jnp.int32), pos])[None, :]
        o = sink_attention(q[:, None], kk[:, None], vv[:, None], q_pos, k_pos, sinks)[:, 0]
        new_k = kk[:, kk.shape[1] - WINDOW:]
        new_v = vv[:, vv.shape[1] - WINDOW:]
    return o @ w_o, new_k, new_v


def run_trunk(x, start, conv_ctx, pool_ctx, k_buf, v_buf, ffn_ctx,
              norm_mix, w_in_ab, conv_a, w_pool, pool_scale, w_out_ab,
              w_qkv, b_qkv, sinks, w_o, norm_ffn, w_up, conv_ffn, w_down, norm_final, banded):
    T = x.shape[1]
    pos = start + jnp.arange(T, dtype=jnp.int32)
    new_conv, new_pool, new_k, new_v, new_ffn = [], [], [], [], []
    for layer in range(DEPTH):
        i = layer // 2
        h = rms_norm(x, norm_mix[layer])
        if layer % 2 == 0:
            proj = h @ w_in_ab[i]
            gate_b = proj[..., :D_A]
            gate_c = proj[..., D_A:2 * D_A]
            x_a = proj[..., 2 * D_A:3 * D_A]
            p_b = proj[..., 3 * D_A:]
            conv_y, nc = causal_dwconv(conv_ctx[i], gate_c * x_a, conv_a[i])
            y_b, npl = pool_mixer(p_b, pool_ctx[i], pos, w_pool[i], pool_scale[i])
            x = x + jnp.concatenate([gate_b * conv_y, y_b], axis=-1) @ w_out_ab[i]
            new_conv.append(nc)
            new_pool.append(npl)
        else:
            kb = None if k_buf is None else k_buf[i]
            vb = None if v_buf is None else v_buf[i]
            att, nk, nv = swa_mixer(h, pos, w_qkv[i], b_qkv[i], sinks[i], w_o[i], kb, vb, banded)
            x = x + att
            new_k.append(nk)
            new_v.append(nv)
        h = rms_norm(x, norm_ffn[layer])
        u, nf = causal_dwconv(ffn_ctx[layer], h @ w_up[layer], conv_ffn[layer])
        x = x + (jax.nn.silu(u[..., :D_FF]) * u[..., D_FF:]) @ w_down[layer]
        new_ffn.append(nf)
    y = rms_norm(x, norm_final)
    return y, jnp.stack(new_conv), jnp.stack(new_pool), jnp.stack(new_k), jnp.stack(new_v), jnp.stack(new_ffn)


def setup_inputs(seed: int = 0) -> dict:
    key = jax.random.key(seed)
    ks = jax.random.split(key, 22)
    f32 = jnp.float32

    def nrm(k, shape, scale):
        return jax.random.normal(k, shape, f32) * scale

    return {
        'x_prompt': nrm(ks[0], (BATCH, SEQ, D_MODEL), 1.0),
        'x_sample': nrm(ks[1], (DEC_BATCH, DEC_SEQ, D_MODEL), 1.0),
        'state_conv_a': nrm(ks[2], (N_EVEN, DEC_BATCH, SHORT_CONV_W - 1, D_A), 1.0),
        'state_pool': nrm(ks[3], (N_EVEN, DEC_BATCH, POOL_CTX, D_B), 1.0),
        'cache_win_k': nrm(ks[4], (N_ODD, DEC_BATCH, WINDOW, N_KV_HEADS, HEAD_DIM), 1.0),
        'cache_win_v': nrm(ks[5], (N_ODD, DEC_BATCH, WINDOW, N_KV_HEADS, HEAD_DIM), 1.0),
        'state_ffn_conv': nrm(ks[6], (DEPTH, DEC_BATCH, FFN_CONV_W - 1, 2 * D_FF), 1.0),
        'norm_mix': 1.0 + nrm(ks[7], (DEPTH, D_MODEL), 0.02),
        'w_in_ab': nrm(ks[8], (N_EVEN, D_MODEL, 3 * D_A + D_B), D_MODEL ** -0.5),
        'conv_a': nrm(ks[9], (N_EVEN, SHORT_CONV_W, D_A), SHORT_CONV_W ** -0.5),
        'w_pool': nrm(ks[10], (N_EVEN, N_POOL_GROUPS, POOL_GROUP, POOL_GROUP), POOL_GROUP ** -0.5),
        'pool_scale': 1.0 + nrm(ks[11], (N_EVEN, D_B), 0.02),
        'w_out_ab': nrm(ks[12], (N_EVEN, D_A + D_B, D_MODEL), (D_A + D_B) ** -0.5),
        'w_qkv': nrm(ks[13], (N_ODD, D_MODEL, QKV_WIDTH), D_MODEL ** -0.5),
        'b_qkv': nrm(ks[14], (N_ODD, QKV_WIDTH), 0.02),
        'sinks': nrm(ks[15], (N_ODD, N_HEADS), 1.0),
        'w_o': nrm(ks[16], (N_ODD, N_HEADS * HEAD_DIM, D_MODEL), (N_HEADS * HEAD_DIM) ** -0.5),
        'norm_ffn': 1.0 + nrm(ks[17], (DEPTH, D_MODEL), 0.02),
        'w_up': nrm(ks[18], (DEPTH, D_MODEL, 2 * D_FF), D_MODEL ** -0.5),
        'conv_ffn': nrm(ks[19], (DEPTH, FFN_CONV_W, 2 * D_FF), FFN_CONV_W ** -0.5),
        'w_down': nrm(ks[20], (DEPTH, D_FF, D_MODEL), D_FF ** -0.5),
        'norm_final': 1.0 + nrm(ks[21], (D_MODEL,), 0.02),
    }


def reference(x_prompt, x_sample, state_conv_a, state_pool, cache_win_k, cache_win_v, state_ffn_conv,
              norm_mix, w_in_ab, conv_a, w_pool, pool_scale, w_out_ab,
              w_qkv, b_qkv, sinks, w_o, norm_ffn, w_up, conv_ffn, w_down, norm_final):
    dt = x_prompt.dtype
    bp = x_prompt.shape[0]
    zero_conv = jnp.zeros((N_EVEN, bp, SHORT_CONV_W - 1, D_A), dt)
    zero_pool = jnp.zeros((N_EVEN, bp, POOL_CTX, D_B), dt)
    zero_ffn = jnp.zeros((DEPTH, bp, FFN_CONV_W - 1, 2 * D_FF), dt)
    y_prompt, ca_p, pl_p, k_p, v_p, f_p = run_trunk(
        x_prompt, 0, zero_conv, zero_pool, None, None, zero_ffn,
        norm_mix, w_in_ab, conv_a, w_pool, pool_scale, w_out_ab,
        w_qkv, b_qkv, sinks, w_o, norm_ffn, w_up, conv_ffn, w_down, norm_final, True)
    y_sample, ca_s, pl_s, k_s, v_s, f_s = run_trunk(
        x_sample, PAST_LEN, state_conv_a, state_pool, cache_win_k, cache_win_v, state_ffn_conv,
        norm_mix, w_in_ab, conv_a, w_pool, pool_scale, w_out_ab,
        w_qkv, b_qkv, sinks, w_o, norm_ffn, w_up, conv_ffn, w_down, norm_final, False)
    return (y_prompt, y_sample, ca_p, ca_s, pl_p, pl_s, k_p, k_s, v_p, v_s, f_p, f_s)
```

```python
import functools

import jax
import jax.numpy as jnp
from jax import lax
from jax.experimental import pallas as pl
from jax.experimental.pallas import tpu as pltpu

D_MODEL = 4096
SEQ = 2048
DEC_BATCH = 32
DEC_SEQ = 4
PAST_LEN = 16384
D_A = D_MODEL // 2
D_B = D_MODEL // 2
POOL_WINDOWS = (2, 4, 8, 16)
POOL_GROUP = D_B // len(POOL_WINDOWS)
POOL_CTX = max(POOL_WINDOWS) - 1
HEAD_DIM = 64
N_HEADS = D_MODEL // HEAD_DIM
N_KV_HEADS = 8
GQA_GROUP = N_HEADS // N_KV_HEADS
WINDOW = 128
ROPE_THETA = 10000.0
D_FF = 11008
EPS = 1e-6

F32 = jnp.float32
BF16 = jnp.bfloat16

LANES = 128
SUBLANES = 8
VMEM_LIMIT = 56 * 1024 * 1024
Q_COLS = N_HEADS * HEAD_DIM
KV_DUP = N_KV_HEADS * LANES


def _params():
    return pltpu.CompilerParams(
        dimension_semantics=("arbitrary", "arbitrary"), vmem_limit_bytes=VMEM_LIMIT)


def _params1():
    return pltpu.CompilerParams(
        dimension_semantics=("arbitrary",), vmem_limit_bytes=VMEM_LIMIT)


def _dot(a, b):
    return jnp.dot(a, b, preferred_element_type=F32)


def _div_pow2(x, d):
    assert d & (d - 1) == 0
    return lax.shift_right_logical(x, d.bit_length() - 1)


def _norm_kernel(x_ref, g_ref, o_ref):
    x = x_ref[...]
    y = x * lax.rsqrt(jnp.mean(x * x, axis=-1, keepdims=True) + EPS)
    o_ref[...] = (y * g_ref[...]).astype(o_ref.dtype)


def _rmsnorm(x, g, out_dtype, tr):
    rows, d = x.shape
    return pl.pallas_call(
        _norm_kernel,
        grid=(rows // tr,),
        in_specs=[pl.BlockSpec((tr, d), lambda i: (i, 0)),
                  pl.BlockSpec((1, d), lambda i: (0, 0))],
        out_specs=pl.BlockSpec((tr, d), lambda i: (i, 0)),
        out_shape=jax.ShapeDtypeStruct((rows, d), out_dtype),
        compiler_params=_params1(),
        name="rmsnorm",
    )(x, g.reshape(1, d))


def _fill_header(z_ref, ctx_ref, carry_ref, n, first, hdr):
    @pl.when(first)
    def _():
        z_ref[0:hdr, :] = ctx_ref[...]

    @pl.when(jnp.logical_not(first))
    def _():
        z_ref[0:hdr, :] = carry_ref[n]


def _emit_tail(z_ref, carry_ref, tail_ref, n, tm, hdr):
    tail = z_ref[tm:tm + hdr, :]
    carry_ref[n] = tail
    tail_ref[...] = tail


def _mix_a_kernel(h_ref, wb_ref, wc_ref, wx_ref, cw_ref, ctx_ref, y_ref, tail_ref, z_ref, carry_ref,
                  *, tm, hdr, shift, tps):
    m = pl.program_id(0)
    n = pl.program_id(1)
    h = h_ref[...]
    gate_b = _dot(h, wb_ref[...])
    u = _dot(h, wc_ref[...]) * _dot(h, wx_ref[...])
    z_ref[hdr:hdr + tm, :] = u
    _fill_header(z_ref, ctx_ref, carry_ref, n, m % tps == 0, hdr)
    cw = cw_ref[...]
    conv = z_ref[hdr - 2 * shift:hdr - 2 * shift + tm, :] * cw[0:1, :]
    conv = conv + z_ref[hdr - shift:hdr - shift + tm, :] * cw[1:2, :]
    conv = conv + u * cw[2:3, :]
    y_ref[...] = (gate_b * conv).astype(y_ref.dtype)
    _emit_tail(z_ref, carry_ref, tail_ref, n, tm, hdr)


def _mix_a(h, w_in, conv_w, ctx, *, tm, tn, hdr, shift, tps):
    rows = h.shape[0]
    nm, nn = rows // tm, D_A // tn
    kern = functools.partial(_mix_a_kernel, tm=tm, hdr=hdr, shift=shift, tps=tps)
    return pl.pallas_call(
        kern,
        grid=(nm, nn),
        in_specs=[
            pl.BlockSpec((tm, D_MODEL), lambda m, n: (m, 0)),
            pl.BlockSpec((D_MODEL, tn), lambda m, n: (0, n)),
            pl.BlockSpec((D_MODEL, tn), lambda m, n: (0, n + nn)),
            pl.BlockSpec((D_MODEL, tn), lambda m, n: (0, n + 2 * nn)),
            pl.BlockSpec((3, tn), lambda m, n: (0, n)),
            pl.BlockSpec((hdr, tn), lambda m, n: (m // tps, n)),
        ],
        out_specs=[
            pl.BlockSpec((tm, tn), lambda m, n: (m, n)),
            pl.BlockSpec((hdr, tn), lambda m, n: (m, n)),
        ],
        out_shape=[
            jax.ShapeDtypeStruct((rows, D_A), BF16),
            jax.ShapeDtypeStruct((nm * hdr, D_A), F32),
        ],
        scratch_shapes=[pltpu.VMEM((hdr + tm, tn), F32), pltpu.VMEM((nn, hdr, tn), F32)],
        compiler_params=_params(),
        name="mix_a",
    )(h, w_in, w_in, w_in, conv_w, ctx)


def _mix_b_kernel(h_ref, wp_ref, wg_ref, sc_ref, ctx_ref, y_ref, tail_ref, z_ref, carry_ref,
                  *, tm, hdr, shift, tps, pos0):
    m = pl.program_id(0)
    g = pl.program_id(1)
    p = _dot(h_ref[...], wp_ref[...])
    z_ref[hdr:hdr + tm, :] = p
    _fill_header(z_ref, ctx_ref, carry_ref, g, m % tps == 0, hdr)
    row = lax.broadcasted_iota(jnp.int32, (tm, 1), 0) + (m % tps) * tm
    pos = pos0 + _div_pow2(row, shift)
    for gi, win in enumerate(POOL_WINDOWS):
        @pl.when(g == gi)
        def _(win=win):
            acc = p
            for i in range(1, win):
                acc = acc + z_ref[hdr - i * shift:hdr - i * shift + tm, :]
            cnt = jnp.minimum(pos + 1, win).astype(F32)
            pooled = acc / cnt - p
            yb = _dot(pooled.astype(BF16), wg_ref[...]) * sc_ref[...]
            y_ref[...] = yb.astype(y_ref.dtype)
    _emit_tail(z_ref, carry_ref, tail_ref, g, tm, hdr)


def _mix_b(h, w_in, w_grp, scale, ctx, *, tm, hdr, shift, tps, pos0):
    rows = h.shape[0]
    nm, ng, tn = rows // tm, len(POOL_WINDOWS), POOL_GROUP
    col0 = 3 * D_A // tn
    kern = functools.partial(_mix_b_kernel, tm=tm, hdr=hdr, shift=shift, tps=tps, pos0=pos0)
    return pl.pallas_call(
        kern,
        grid=(nm, ng),
        in_specs=[
            pl.BlockSpec((tm, D_MODEL), lambda m, g: (m, 0)),
            pl.BlockSpec((D_MODEL, tn), lambda m, g: (0, col0 + g)),
            pl.BlockSpec((None, tn, tn), lambda m, g: (g, 0, 0)),
            pl.BlockSpec((1, tn), lambda m, g: (0, g)),
            pl.BlockSpec((hdr, tn), lambda m, g: (m // tps, g)),
        ],
        out_specs=[
            pl.BlockSpec((tm, tn), lambda m, g: (m, g)),
            pl.BlockSpec((hdr, tn), lambda m, g: (m, g)),
        ],
        out_shape=[
            jax.ShapeDtypeStruct((rows, D_B), BF16),
            jax.ShapeDtypeStruct((nm * hdr, D_B), F32),
        ],
        scratch_shapes=[pltpu.VMEM((hdr + tm, tn), F32), pltpu.VMEM((ng, hdr, tn), F32)],
        compiler_params=_params(),
        name="mix_b",
    )(h, w_in, w_grp, scale, ctx)


def _res_mm_kernel(*refs, n_lhs):
    x_ref, o_ref = refs[0], refs[-1]
    acc = x_ref[...]
    for i in range(n_lhs):
        acc = acc + _dot(refs[1 + i][...], refs[1 + n_lhs + i][...])
    o_ref[...] = acc


def _res_mm(x, lhs_list, w, *, tm, tn, name):
    rows, d_out = x.shape
    n_lhs = len(lhs_list)
    kdim = lhs_list[0].shape[1]
    in_specs = [pl.BlockSpec((tm, tn), lambda m, n: (m, n))]
    in_specs += [pl.BlockSpec((tm, kdim), lambda m, n: (m, 0)) for _ in lhs_list]
    in_specs += [pl.BlockSpec((kdim, tn), lambda m, n, i=i: (i, n)) for i in range(n_lhs)]
    return pl.pallas_call(
        functools.partial(_res_mm_kernel, n_lhs=n_lhs),
        grid=(rows // tm, d_out // tn),
        in_specs=in_specs,
        out_specs=pl.BlockSpec((tm, tn), lambda m, n: (m, n)),
        out_shape=jax.ShapeDtypeStruct((rows, d_out), F32),
        compiler_params=_params(),
        name=name,
    )(x, *lhs_list, *([w] * n_lhs))


def _rope_table_kernel(inv_ref, cos_ref, sin_ref, *, pos0, shift):
    rows = cos_ref.shape[0]
    row = lax.broadcasted_iota(jnp.int32, (rows, LANES), 0)
    lane = lax.broadcasted_iota(jnp.int32, (rows, LANES), 1)
    pos = (pos0 + _div_pow2(row, shift)).astype(F32)
    ang = pos * inv_ref[...]
    first_half = (lane & (HEAD_DIM - 1)) < (HEAD_DIM // 2)
    cos_ref[...] = jnp.cos(ang)
    sin = jnp.sin(ang)
    sin_ref[...] = jnp.where(first_half, -sin, sin)


def _rope_table(rows, pos0, shift):
    half = HEAD_DIM // 2
    inv = ROPE_THETA ** (-jnp.arange(half, dtype=F32) / half)
    inv = jnp.tile(inv, LANES // half).reshape(1, LANES)
    return pl.pallas_call(
        functools.partial(_rope_table_kernel, pos0=pos0, shift=shift),
        out_shape=[jax.ShapeDtypeStruct((rows, LANES), F32)] * 2,
        name="rope_table",
    )(inv)


def _qkv_kernel(h_ref, w_ref, b_ref, cos_ref, sin_ref, q_ref, kv_ref, *, nq, nrope):
    n = pl.program_id(1)
    tn = w_ref.shape[1]
    acc = _dot(h_ref[...], w_ref[...]) + b_ref[...]
    reps = tn // LANES
    cos = jnp.concatenate([cos_ref[...]] * reps, axis=1)
    sin = jnp.concatenate([sin_ref[...]] * reps, axis=1)
    lane = lax.broadcasted_iota(jnp.int32, acc.shape, 1)
    first_half = (lane & (HEAD_DIM - 1)) < (HEAD_DIM // 2)
    half = HEAD_DIM // 2
    partner = jnp.where(first_half, pltpu.roll(acc, tn - half, 1), pltpu.roll(acc, half, 1))
    roped = acc * cos + partner * sin

    @pl.when(n < nq)
    def _():
        q_ref[...] = (roped * (HEAD_DIM ** -0.5)).astype(q_ref.dtype)

    @pl.when(jnp.logical_and(n >= nq, n < nrope))
    def _():
        kv_ref[...] = roped

    @pl.when(n >= nrope)
    def _():
        kv_ref[...] = acc


def _qkv(h, w, b, cos_t, sin_t, *, tm, tn, tps):
    rows = h.shape[0]
    ncols = w.shape[1]
    nq = Q_COLS // tn
    nrope = (Q_COLS + KV_DUP) // tn
    return pl.pallas_call(
        functools.partial(_qkv_kernel, nq=nq, nrope=nrope),
        grid=(rows // tm, ncols // tn),
        in_specs=[
            pl.BlockSpec((tm, D_MODEL), lambda m, n: (m, 0)),
            pl.BlockSpec((D_MODEL, tn), lambda m, n: (0, n)),
            pl.BlockSpec((1, tn), lambda m, n: (0, n)),
            pl.BlockSpec((tm, LANES), lambda m, n: (m % tps, 0)),
            pl.BlockSpec((tm, LANES), lambda m, n: (m % tps, 0)),
        ],
        out_specs=[
            pl.BlockSpec((tm, tn), lambda m, n: (m, jnp.minimum(n, nq - 1))),
            pl.BlockSpec((tm, tn), lambda m, n: (m, jnp.maximum(n - nq, 0))),
        ],
        out_shape=[
            jax.ShapeDtypeStruct((rows, Q_COLS), BF16),
            jax.ShapeDtypeStruct((rows, 2 * KV_DUP), F32),
        ],
        compiler_params=_params(),
        name="qkv_rope",
    )(h, w, b, cos_t, sin_t)


def _softmax_pair(s, sink, mask):
    keys = s.shape[1] // 2
    ps, ds = [], []
    for i in range(2):
        si = jnp.where(mask, s[:, i * keys:(i + 1) * keys], -jnp.inf)
        sk = sink[:, i * keys:i * keys + 1]
        mx = jnp.maximum(jnp.max(si, axis=-1, keepdims=True), sk)
        pi = jnp.exp(si - mx)
        ps.append(pi)
        ds.append(jnp.sum(pi, axis=-1, keepdims=True) + jnp.exp(sk - mx))
    return jnp.concatenate(ps, axis=1), ds


def _block_diag(x2):
    lane = lax.broadcasted_iota(jnp.int32, x2.shape, 1)
    lo = jnp.where(lane < HEAD_DIM, x2, 0.0)
    hi = jnp.where(lane >= HEAD_DIM, x2, 0.0)
    return jnp.concatenate([lo, hi], axis=0).astype(BF16)


def _attn_kernel(q_ref, kvp_ref, kvc_ref, sink_ref, o_ref, *, blocks_per_seq):
    i = pl.program_id(0)
    no_prev = jnp.where((i % blocks_per_seq) > 0, 0, 2 * WINDOW)
    qi = lax.broadcasted_iota(jnp.int32, (WINDOW, 2 * WINDOW), 0)
    kj = lax.broadcasted_iota(jnp.int32, (WINDOW, 2 * WINDOW), 1)
    mask = ((kj < WINDOW) & (kj > qi + no_prev)) | ((kj >= WINDOW) & (kj - WINDOW <= qi))
    lane = lax.broadcasted_iota(jnp.int32, (WINDOW, LANES), 1)
    pairs = GQA_GROUP // 2
    for kh in range(N_KV_HEADS):
        kc = slice(kh * LANES, (kh + 1) * LANES)
        vc = slice(KV_DUP + kh * LANES, KV_DUP + (kh + 1) * LANES)
        kbd = _block_diag(jnp.concatenate([kvp_ref[:, kc], kvc_ref[:, kc]], axis=0))
        vbd = _block_diag(jnp.concatenate([kvp_ref[:, vc], kvc_ref[:, vc]], axis=0))
        for pi in range(pairs):
            pair = kh * pairs + pi
            qc = slice(pair * LANES, (pair + 1) * LANES)
            s = lax.dot_general(q_ref[:, qc], kbd, (((1,), (1,)), ((), ())),
                                preferred_element_type=F32)
            p, ds = _softmax_pair(s, sink_ref[pair:pair + 1, :], mask)
            o = _dot(p.astype(BF16), vbd)
            o_ref[:, qc] = (o / jnp.where(lane < HEAD_DIM, ds[0], ds[1])).astype(o_ref.dtype)


def _attn_prompt(q, kv, sink_t):
    rows = q.shape[0]
    bps = SEQ // WINDOW
    return pl.pallas_call(
        functools.partial(_attn_kernel, blocks_per_seq=bps),
        grid=(rows // WINDOW,),
        in_specs=[
            pl.BlockSpec((WINDOW, Q_COLS), lambda i: (i, 0)),
            pl.BlockSpec((WINDOW, 2 * KV_DUP), lambda i: (jnp.where(i % bps == 0, i, i - 1), 0)),
            pl.BlockSpec((WINDOW, 2 * KV_DUP), lambda i: (i, 0)),
            pl.BlockSpec(sink_t.shape, lambda i: (0, 0)),
        ],
        out_specs=pl.BlockSpec((WINDOW, Q_COLS), lambda i: (i, 0)),
        out_shape=jax.ShapeDtypeStruct((rows, Q_COLS), BF16),
        compiler_params=_params1(),
        name="attn_banded",
    )(q, kv, kv, sink_t)


def _attn_sample_kernel(q_ref, k_ref, v_ref, sink_ref, o_ref, *, n_keys, kpos0):
    rows = GQA_GROUP * DEC_SEQ
    keys = k_ref.shape[2]
    t = lax.broadcasted_iota(jnp.int32, (rows, keys), 0) % DEC_SEQ
    j = lax.broadcasted_iota(jnp.int32, (rows, keys), 1)
    mask = (j > t) & (j <= t + WINDOW) & (j < n_keys) & (j + kpos0 >= 0)
    for kh in range(N_KV_HEADS):
        s = lax.dot_general(q_ref[0, kh], k_ref[0, kh].astype(BF16), (((1,), (1,)), ((), ())),
                            preferred_element_type=F32)
        s = jnp.where(mask, s, -jnp.inf)
        sk = sink_ref[kh][:, 0:1]
        mx = jnp.maximum(jnp.max(s, axis=-1, keepdims=True), sk)
        p = jnp.exp(s - mx)
        d = jnp.sum(p, axis=-1, keepdims=True) + jnp.exp(sk - mx)
        o = _dot(p.astype(BF16), v_ref[0, kh].astype(BF16))
        o_ref[0, kh] = (o / d).astype(o_ref.dtype)


def _attn_sample(q, k, v, sink_s, n_keys):
    nb, nkv, rows, d = q.shape
    keys = k.shape[2]
    return pl.pallas_call(
        functools.partial(_attn_sample_kernel, n_keys=n_keys, kpos0=PAST_LEN - WINDOW),
        grid=(nb,),
        in_specs=[
            pl.BlockSpec((1, nkv, rows, d), lambda b: (b, 0, 0, 0)),
            pl.BlockSpec((1, nkv, keys, d), lambda b: (b, 0, 0, 0)),
            pl.BlockSpec((1, nkv, keys, d), lambda b: (b, 0, 0, 0)),
            pl.BlockSpec(sink_s.shape, lambda b: (0, 0, 0)),
        ],
        out_specs=pl.BlockSpec((1, nkv, rows, d), lambda b: (b, 0, 0, 0)),
        out_shape=jax.ShapeDtypeStruct(q.shape, BF16),
        compiler_params=_params1(),
        name="attn_sample",
    )(q, k, v, sink_s)


def _ffn_up_kernel(h_ref, wg_ref, wu_ref, cwg_ref, cwu_ref, ctxg_ref, ctxu_ref,
                   act_ref, tailg_ref, tailu_ref, zg_ref, zu_ref, carryg_ref, carryu_ref,
                   *, tm, hdr, shift, tps):
    m = pl.program_id(0)
    n = pl.program_id(1)
    first = m % tps == 0
    h = h_ref[...]

    def conv(w_ref, cw_ref, ctx_ref, z_ref, carry_ref, tail_ref):
        up = _dot(h, w_ref[...])
        z_ref[hdr:hdr + tm, :] = up
        _fill_header(z_ref, ctx_ref, carry_ref, n, first, hdr)
        cw = cw_ref[...]
        y = z_ref[hdr - 2 * shift:hdr - 2 * shift + tm, :] * cw[0:1, :]
        y = y + z_ref[hdr - shift:hdr - shift + tm, :] * cw[1:2, :]
        y = y + up * cw[2:3, :]
        _emit_tail(z_ref, carry_ref, tail_ref, n, tm, hdr)
        return y

    gate = conv(wg_ref, cwg_ref, ctxg_ref, zg_ref, carryg_ref, tailg_ref)
    lin = conv(wu_ref, cwu_ref, ctxu_ref, zu_ref, carryu_ref, tailu_ref)
    silu = gate / (1.0 + jnp.exp(-gate))
    act_ref[...] = (silu * lin).astype(act_ref.dtype)


def _ffn_up(h, w_up, conv_w, ctx, *, tm, tn, hdr, shift, tps):
    rows = h.shape[0]
    nm, nn = rows // tm, D_FF // tn
    kern = functools.partial(_ffn_up_kernel, tm=tm, hdr=hdr, shift=shift, tps=tps)
    lo = lambda m, n: (0, n)
    hi = lambda m, n: (0, n + nn)
    tail_shape = jax.ShapeDtypeStruct((nm * hdr, D_FF), F32)
    return pl.pallas_call(
        kern,
        grid=(nm, nn),
        in_specs=[
            pl.BlockSpec((tm, D_MODEL), lambda m, n: (m, 0)),
            pl.BlockSpec((D_MODEL, tn), lo),
            pl.BlockSpec((D_MODEL, tn), hi),
            pl.BlockSpec((3, tn), lo),
            pl.BlockSpec((3, tn), hi),
            pl.BlockSpec((hdr, tn), lambda m, n: (m // tps, n)),
            pl.BlockSpec((hdr, tn), lambda m, n: (m // tps, n + nn)),
        ],
        out_specs=[
            pl.BlockSpec((tm, tn), lambda m, n: (m, n)),
            pl.BlockSpec((hdr, tn), lambda m, n: (m, n)),
            pl.BlockSpec((hdr, tn), lambda m, n: (m, n)),
        ],
        out_shape=[jax.ShapeDtypeStruct((rows, D_FF), BF16), tail_shape, tail_shape],
        scratch_shapes=[pltpu.VMEM((hdr + tm, tn), F32), pltpu.VMEM((hdr + tm, tn), F32),
                        pltpu.VMEM((nn, hdr, tn), F32), pltpu.VMEM((nn, hdr, tn), F32)],
        compiler_params=_params(),
        name="ffn_up",
    )(h, w_up, w_up, conv_w, conv_w, ctx, ctx)


class _Layout:
    def __init__(self, *, prompt):
        if prompt:
            self.tm, self.tps, self.shift, self.pos0 = 1024, SEQ // 1024, 1, 0
            self.tm_down, self.tr = 512, 512
        else:
            self.tm, self.tps, self.shift, self.pos0 = DEC_BATCH * DEC_SEQ, 1, DEC_BATCH, PAST_LEN
            self.tm_down, self.tr = self.tm, self.tm
        unit = SUBLANES if prompt else DEC_BATCH
        self.hdr_conv = max(SUBLANES, 2 * unit)
        self.hdr_pool = (POOL_CTX + 1) * (1 if prompt else DEC_BATCH)


def _ffn(x, lay, norm_g, w_up, conv_w, w_down, ctx):
    h = _rmsnorm(x, norm_g, BF16, lay.tr)
    act, tail_g, tail_u = _ffn_up(h, w_up, conv_w, ctx, tm=lay.tm, tn=256, hdr=lay.hdr_conv,
                                  shift=lay.shift, tps=lay.tps)
    x = _res_mm(x, [act], w_down, tm=lay.tm_down, tn=512, name="ffn_down")
    return x, jnp.concatenate([tail_g, tail_u], axis=1)


def _trunk(x, lay, ctx_a, ctx_pool, ctx_ffn, attn_fn, wts):
    h = _rmsnorm(x, wts["norm_mix"][0], BF16, lay.tr)
    y_a, tail_a = _mix_a(h, wts["w_in"], wts["conv_a"], ctx_a, tm=lay.tm, tn=256,
                         hdr=lay.hdr_conv, shift=lay.shift, tps=lay.tps)
    y_b, tail_p = _mix_b(h, wts["w_in"], wts["w_pool"], wts["pool_scale"], ctx_pool, tm=lay.tm,
                         hdr=lay.hdr_pool, shift=lay.shift, tps=lay.tps, pos0=lay.pos0)
    x = _res_mm(x, [y_a, y_b], wts["w_out"], tm=lay.tm, tn=512, name="out_proj")
    x, tail_f0 = _ffn(x, lay, wts["norm_ffn"][0], wts["w_up"][0], wts["conv_ffn"][0],
                      wts["w_down"][0], ctx_ffn[0])

    h = _rmsnorm(x, wts["norm_mix"][1], BF16, lay.tr)
    cos_t, sin_t = _rope_table(lay.tm * lay.tps, lay.pos0, lay.shift)
    q, kv = _qkv(h, wts["w_qkv"], wts["b_qkv"], cos_t, sin_t, tm=lay.tm, tn=512, tps=lay.tps)
    o, new_k, new_v = attn_fn(q, kv)
    x = _res_mm(x, [o], wts["w_o"], tm=lay.tm, tn=512, name="attn_out")
    x, tail_f1 = _ffn(x, lay, wts["norm_ffn"][1], wts["w_up"][1], wts["conv_ffn"][1],
                      wts["w_down"][1], ctx_ffn[1])
    y = _rmsnorm(x, wts["norm_final"], F32, lay.tr)
    return y, tail_a, tail_p, new_k, new_v, (tail_f0, tail_f1)


def _dup_heads(w):
    lead = w.shape[:-1]
    w = w.reshape(lead + (N_KV_HEADS, 1, HEAD_DIM))
    return jnp.broadcast_to(w, lead + (N_KV_HEADS, 2, HEAD_DIM)).reshape(lead + (KV_DUP,))


def _undup_heads(kv_half):
    return kv_half.reshape(kv_half.shape[0], N_KV_HEADS, 2, HEAD_DIM)[:, :, 0, :]


def kernel(x_prompt, x_sample, state_conv_a, state_pool, cache_win_k, cache_win_v, state_ffn_conv,
           norm_mix, w_in_ab, conv_a, w_pool, pool_scale, w_out_ab,
           w_qkv, b_qkv, sinks, w_o, norm_ffn, w_up, conv_ffn, w_down, norm_final):
    bp = x_prompt.shape[0]
    nkq = N_KV_HEADS * HEAD_DIM

    wq, wk, wv = (w_qkv[0][:, :Q_COLS], w_qkv[0][:, Q_COLS:Q_COLS + nkq], w_qkv[0][:, Q_COLS + nkq:])
    bq, bk, bv = (b_qkv[0][:Q_COLS], b_qkv[0][Q_COLS:Q_COLS + nkq], b_qkv[0][Q_COLS + nkq:])
    wts = {
        "norm_mix": norm_mix, "norm_ffn": norm_ffn, "norm_final": norm_final,
        "w_in": w_in_ab[0].astype(BF16), "conv_a": conv_a[0],
        "w_pool": w_pool[0].astype(BF16), "pool_scale": pool_scale[0].reshape(1, D_B),
        "w_out": w_out_ab[0].astype(BF16),
        "w_qkv": jnp.concatenate([wq, _dup_heads(wk), _dup_heads(wv)], axis=1).astype(BF16),
        "b_qkv": jnp.concatenate([bq, _dup_heads(bk), _dup_heads(bv)]).reshape(1, -1),
        "w_o": w_o[0].astype(BF16),
        "w_up": [w_up[i].astype(BF16) for i in range(2)],
        "conv_ffn": [conv_ffn[i] for i in range(2)],
        "w_down": [w_down[i].astype(BF16) for i in range(2)],
    }
    sink = sinks[0].astype(F32)
    sink_t = jnp.repeat(sink.reshape(N_HEADS // 2, 2), 2 * WINDOW, axis=1)
    sink_s = jnp.broadcast_to(
        jnp.repeat(sink.reshape(N_KV_HEADS, GQA_GROUP), DEC_SEQ, axis=1)[:, :, None],
        (N_KV_HEADS, GQA_GROUP * DEC_SEQ, LANES))

    lay = _Layout(prompt=True)
    rows = bp * SEQ
    tiles = rows // lay.tm
    last_tiles = slice(lay.tps - 1, tiles, lay.tps)

    def attn_prompt(q, kv):
        o = _attn_prompt(q, kv, sink_t)
        kv_last = kv.reshape(bp, SEQ, 2 * KV_DUP)[:, SEQ - WINDOW:].reshape(bp * WINDOW, 2 * KV_DUP)
        new_k = _undup_heads(kv_last[:, :KV_DUP]).reshape(bp, WINDOW, N_KV_HEADS, HEAD_DIM)
        new_v = _undup_heads(kv_last[:, KV_DUP:]).reshape(bp, WINDOW, N_KV_HEADS, HEAD_DIM)
        return o, new_k, new_v

    y_p, tail_a, tail_p, k_p, v_p, tails_f = _trunk(
        x_prompt.reshape(rows, D_MODEL), lay,
        jnp.zeros((bp * lay.hdr_conv, D_A), F32), jnp.zeros((bp * lay.hdr_pool, D_B), F32),
        [jnp.zeros((bp * lay.hdr_conv, 2 * D_FF), F32)] * 2, attn_prompt, wts)

    def last_rows(tail, hdr, keep):
        return tail.reshape(tiles, hdr, -1)[last_tiles, hdr - keep:]

    ca_p = last_rows(tail_a, lay.hdr_conv, 2)[None]
    pl_p = last_rows(tail_p, lay.hdr_pool, POOL_CTX)[None]
    f_p = jnp.stack([last_rows(t, lay.hdr_conv, 2) for t in tails_f])
    y_p = y_p.reshape(bp, SEQ, D_MODEL)

    lay = _Layout(prompt=False)
    nb = x_sample.shape[0]

    def to_tm(a):
        return jnp.swapaxes(a, 0, 1).reshape(-1, a.shape[-1])

    def from_tm(a):
        return jnp.swapaxes(a.reshape(-1, nb, a.shape[-1]), 0, 1)

    def pad_hdr(a, hdr):
        return jnp.concatenate([jnp.zeros((hdr - a.shape[0], a.shape[1]), F32), a], axis=0)

    def attn_sample(q, kv):
        k_new = from_tm(_undup_heads(kv[:, :KV_DUP]).reshape(-1, nkq))
        v_new = from_tm(_undup_heads(kv[:, KV_DUP:]).reshape(-1, nkq))
        kk = jnp.concatenate([cache_win_k[0].reshape(nb, WINDOW, nkq), k_new], axis=1)
        vv = jnp.concatenate([cache_win_v[0].reshape(nb, WINDOW, nkq), v_new], axis=1)
        n_keys = WINDOW + DEC_SEQ
        pad = 2 * WINDOW - n_keys

        def heads_major(a):
            a = jnp.pad(a, ((0, 0), (0, pad), (0, 0)))
            return a.reshape(nb, 2 * WINDOW, N_KV_HEADS, HEAD_DIM).transpose(0, 2, 1, 3)

        qs = from_tm(q).reshape(nb, DEC_SEQ, N_KV_HEADS, GQA_GROUP, HEAD_DIM)
        qs = qs.transpose(0, 2, 3, 1, 4).reshape(nb, N_KV_HEADS, GQA_GROUP * DEC_SEQ, HEAD_DIM)
        o = _attn_sample(qs, heads_major(kk), heads_major(vv), sink_s, n_keys)
        o = o.reshape(nb, N_KV_HEADS, GQA_GROUP, DEC_SEQ, HEAD_DIM).transpose(0, 3, 1, 2, 4)
        o = to_tm(o.reshape(nb, DEC_SEQ, Q_COLS))
        new_k = kk[:, n_keys - WINDOW:].reshape(nb, WINDOW, N_KV_HEADS, HEAD_DIM)
        new_v = vv[:, n_keys - WINDOW:].reshape(nb, WINDOW, N_KV_HEADS, HEAD_DIM)
        return o, new_k, new_v

    y_s, tail_a, tail_p, k_s, v_s, tails_f = _trunk(
        to_tm(x_sample), lay,
        pad_hdr(to_tm(state_conv_a[0]), lay.hdr_conv), pad_hdr(to_tm(state_pool[0]), lay.hdr_pool),
        [pad_hdr(to_tm(state_ffn_conv[i]), lay.hdr_conv) for i in range(2)], attn_sample, wts)

    def last_steps(tail, keep):
        return from_tm(tail[tail.shape[0] - keep * nb:])

    ca_s = last_steps(tail_a, 2)[None]
    pl_s = last_steps(tail_p, POOL_CTX)[None]
    f_s = jnp.stack([last_steps(t, 2) for t in tails_f])
    y_s = from_tm(y_s)

    return (y_p, y_s, ca_p, ca_s, pl_p, pl_s, k_p[None], k_s[None], v_p[None], v_s[None], f_p, f_s)
```

```python
import functools

import jax
import jax.numpy as jnp
from jax import lax
from jax.experimental import pallas as pl
from jax.experimental.pallas import tpu as pltpu

D_MODEL = 4096
SEQ = 2048
DEC_BATCH = 32
DEC_SEQ = 4
PAST_LEN = 16384
D_A = D_MODEL // 2
D_B = D_MODEL // 2
POOL_WINDOWS = (2, 4, 8, 16)
POOL_GROUP = D_B // len(POOL_WINDOWS)
POOL_CTX = max(POOL_WINDOWS) - 1
HEAD_DIM = 64
N_HEADS = D_MODEL // HEAD_DIM
N_KV_HEADS = 8
GQA_GROUP = N_HEADS // N_KV_HEADS
WINDOW = 128
ROPE_THETA = 10000.0
D_FF = 11008
EPS = 1e-6

F32 = jnp.float32
BF16 = jnp.bfloat16

LANES = 128
SUBLANES = 8
VMEM_LIMIT = 56 * 1024 * 1024
Q_COLS = N_HEADS * HEAD_DIM
KV_DUP = N_KV_HEADS * LANES


def _params():
    return pltpu.CompilerParams(
        dimension_semantics=("arbitrary", "arbitrary"), vmem_limit_bytes=VMEM_LIMIT)


def _params1():
    return pltpu.CompilerParams(
        dimension_semantics=("arbitrary",), vmem_limit_bytes=VMEM_LIMIT)


def _dot(a, b):
    return jnp.dot(a, b, preferred_element_type=F32)


def _div_pow2(x, d):
    assert d & (d - 1) == 0
    return lax.shift_right_logical(x, d.bit_length() - 1)


def _norm_kernel(x_ref, g_ref, o_ref):
    x = x_ref[...]
    y = x * lax.rsqrt(jnp.mean(x * x, axis=-1, keepdims=True) + EPS)
    o_ref[...] = (y * g_ref[...]).astype(o_ref.dtype)


def _rmsnorm(x, g, out_dtype, tr):
    rows, d = x.shape
    return pl.pallas_call(
        _norm_kernel,
        grid=(rows // tr,),
        in_specs=[pl.BlockSpec((tr, d), lambda i: (i, 0)),
                  pl.BlockSpec((1, d), lambda i: (0, 0))],
        out_specs=pl.BlockSpec((tr, d), lambda i: (i, 0)),
        out_shape=jax.ShapeDtypeStruct((rows, d), out_dtype),
        compiler_params=_params1(),
        name="rmsnorm",
    )(x, g.reshape(1, d))


def _fill_header(z_ref, ctx_ref, carry_ref, n, first, hdr):
    @pl.when(first)
    def _():
        z_ref[0:hdr, :] = ctx_ref[...]

    @pl.when(jnp.logical_not(first))
    def _():
        z_ref[0:hdr, :] = carry_ref[n]


def _emit_tail(z_ref, carry_ref, tail_ref, n, tm, hdr):
    tail = z_ref[tm:tm + hdr, :]
    carry_ref[n] = tail
    tail_ref[...] = tail


def _mix_a_kernel(h_ref, wb_ref, wc_ref, wx_ref, cw_ref, ctx_ref, y_ref, tail_ref, z_ref, carry_ref,
                  *, tm, hdr, shift, tps):
    m = pl.program_id(0)
    n = pl.program_id(1)
    h = h_ref[...]
    gate_b = _dot(h, wb_ref[...])
    u = _dot(h, wc_ref[...]) * _dot(h, wx_ref[...])
    z_ref[hdr:hdr + tm, :] = u
    _fill_header(z_ref, ctx_ref, carry_ref, n, m % tps == 0, hdr)
    cw = cw_ref[...]
    conv = z_ref[hdr - 2 * shift:hdr - 2 * shift + tm, :] * cw[0:1, :]
    conv = conv + z_ref[hdr - shift:hdr - shift + tm, :] * cw[1:2, :]
    conv = conv + u * cw[2:3, :]
    y_ref[...] = (gate_b * conv).astype(y_ref.dtype)
    _emit_tail(z_ref, carry_ref, tail_ref, n, tm, hdr)


def _mix_a(h, w_in, conv_w, ctx, *, tm, tn, hdr, shift, tps):
    rows = h.shape[0]
    nm, nn = rows // tm, D_A // tn
    kern = functools.partial(_mix_a_kernel, tm=tm, hdr=hdr, shift=shift, tps=tps)
    return pl.pallas_call(
        kern,
        grid=(nm, nn),
        in_specs=[
            pl.BlockSpec((tm, D_MODEL), lambda m, n: (m, 0)),
            pl.BlockSpec((D_MODEL, tn), lambda m, n: (0, n)),
            pl.BlockSpec((D_MODEL, tn), lambda m, n: (0, n + nn)),
            pl.BlockSpec((D_MODEL, tn), lambda m, n: (0, n + 2 * nn)),
            pl.BlockSpec((3, tn), lambda m, n: (0, n)),
            pl.BlockSpec((hdr, tn), lambda m, n: (m // tps, n)),
        ],
        out_specs=[
            pl.BlockSpec((tm, tn), lambda m, n: (m, n)),
            pl.BlockSpec((hdr, tn), lambda m, n: (m, n)),
        ],
        out_shape=[
            jax.ShapeDtypeStruct((rows, D_A), BF16),
            jax.ShapeDtypeStruct((nm * hdr, D_A), F32),
        ],
        scratch_shapes=[pltpu.VMEM((hdr + tm, tn), F32), pltpu.VMEM((nn, hdr, tn), F32)],
        compiler_params=_params(),
        name="mix_a",
    )(h, w_in, w_in, w_in, conv_w, ctx)


def _mix_b_kernel(h_ref, wp_ref, wg_ref, sc_ref, ctx_ref, y_ref, tail_ref, z_ref, carry_ref,
                  *, tm, hdr, shift, tps, pos0):
    m = pl.program_id(0)
    g = pl.program_id(1)
    p = _dot(h_ref[...], wp_ref[...])
    z_ref[hdr:hdr + tm, :] = p
    _fill_header(z_ref, ctx_ref, carry_ref, g, m % tps == 0, hdr)
    row = lax.broadcasted_iota(jnp.int32, (tm, 1), 0) + (m % tps) * tm
    pos = pos0 + _div_pow2(row, shift)
    for gi, win in enumerate(POOL_WINDOWS):
        @pl.when(g == gi)
        def _(win=win):
            acc = p
            for i in range(1, win):
                acc = acc + z_ref[hdr - i * shift:hdr - i * shift + tm, :]
            cnt = jnp.minimum(pos + 1, win).astype(F32)
            pooled = acc / cnt - p
            yb = _dot(pooled.astype(BF16), wg_ref[...]) * sc_ref[...]
            y_ref[...] = yb.astype(y_ref.dtype)
    _emit_tail(z_ref, carry_ref, tail_ref, g, tm, hdr)


def _mix_b(h, w_in, w_grp, scale, ctx, *, tm, hdr, shift, tps, pos0):
    rows = h.shape[0]
    nm, ng, tn = rows // tm, len(POOL_WINDOWS), POOL_GROUP
    col0 = 3 * D_A // tn
    kern = functools.partial(_mix_b_kernel, tm=tm, hdr=hdr, shift=shift, tps=tps, pos0=pos0)
    return pl.pallas_call(
        kern,
        grid=(nm, ng),
        in_specs=[
            pl.BlockSpec((tm, D_MODEL), lambda m, g: (m, 0)),
            pl.BlockSpec((D_MODEL, tn), lambda m, g: (0, col0 + g)),
            pl.BlockSpec((None, tn, tn), lambda m, g: (g, 0, 0)),
            pl.BlockSpec((1, tn), lambda m, g: (0, g)),
            pl.BlockSpec((hdr, tn), lambda m, g: (m // tps, g)),
        ],
        out_specs=[
            pl.BlockSpec((tm, tn), lambda m, g: (m, g)),
            pl.BlockSpec((hdr, tn), lambda m, g: (m, g)),
        ],
        out_shape=[
            jax.ShapeDtypeStruct((rows, D_B), BF16),
            jax.ShapeDtypeStruct((nm * hdr, D_B), F32),
        ],
        scratch_shapes=[pltpu.VMEM((hdr + tm, tn), F32), pltpu.VMEM((ng, hdr, tn), F32)],
        compiler_params=_params(),
        name="mix_b",
    )(h, w_in, w_grp, scale, ctx)


def _res_mm_kernel(*refs, n_lhs):
    x_ref, o_ref = refs[0], refs[-1]
    acc = x_ref[...]
    for i in range(n_lhs):
        acc = acc + _dot(refs[1 + i][...], refs[1 + n_lhs + i][...])
    o_ref[...] = acc


def _res_mm(x, lhs_list, w, *, tm, tn, name):
    rows, d_out = x.shape
    n_lhs = len(lhs_list)
    kdim = lhs_list[0].shape[1]
    in_specs = [pl.BlockSpec((tm, tn), lambda m, n: (m, n))]
    in_specs += [pl.BlockSpec((tm, kdim), lambda m, n: (m, 0)) for _ in lhs_list]
    in_specs += [pl.BlockSpec((kdim, tn), lambda m, n, i=i: (i, n)) for i in range(n_lhs)]
    return pl.pallas_call(
        functools.partial(_res_mm_kernel, n_lhs=n_lhs),
        grid=(rows // tm, d_out // tn),
        in_specs=in_specs,
        out_specs=pl.BlockSpec((tm, tn), lambda m, n: (m, n)),
        out_shape=jax.ShapeDtypeStruct((rows, d_out), F32),
        compiler_params=_params(),
        name=name,
    )(x, *lhs_list, *([w] * n_lhs))


def _rope_table_kernel(inv_ref, cos_ref, sin_ref, *, pos0, shift):
    rows = cos_ref.shape[0]
    row = lax.broadcasted_iota(jnp.int32, (rows, LANES), 0)
    lane = lax.broadcasted_iota(jnp.int32, (rows, LANES), 1)
    pos = (pos0 + _div_pow2(row, shift)).astype(F32)
    ang = pos * inv_ref[...]
    first_half = (lane & (HEAD_DIM - 1)) < (HEAD_DIM // 2)
    cos_ref[...] = jnp.cos(ang)
    sin = jnp.sin(ang)
    sin_ref[...] = jnp.where(first_half, -sin, sin)


def _rope_table(rows, pos0, shift):
    half = HEAD_DIM // 2
    inv = ROPE_THETA ** (-jnp.arange(half, dtype=F32) / half)
    inv = jnp.tile(inv, LANES // half).reshape(1, LANES)
    return pl.pallas_call(
        functools.partial(_rope_table_kernel, pos0=pos0, shift=shift),
        out_shape=[jax.ShapeDtypeStruct((rows, LANES), F32)] * 2,
        name="rope_table",
    )(inv)


def _qkv_kernel(h_ref, w_ref, b_ref, cos_ref, sin_ref, q_ref, kv_ref, *, nq, nrope):
    n = pl.program_id(1)
    tn = w_ref.shape[1]
    acc = _dot(h_ref[...], w_ref[...]) + b_ref[...]
    reps = tn // LANES
    cos = jnp.concatenate([cos_ref[...]] * reps, axis=1)
    sin = jnp.concatenate([sin_ref[...]] * reps, axis=1)
    lane = lax.broadcasted_iota(jnp.int32, acc.shape, 1)
    first_half = (lane & (HEAD_DIM - 1)) < (HEAD_DIM // 2)
    half = HEAD_DIM // 2
    partner = jnp.where(first_half, pltpu.roll(acc, tn - half, 1), pltpu.roll(acc, half, 1))
    roped = acc * cos + partner * sin

    @pl.when(n < nq)
    def _():
        q_ref[...] = (roped * (HEAD_DIM ** -0.5)).astype(q_ref.dtype)

    @pl.when(jnp.logical_and(n >= nq, n < nrope))
    def _():
        kv_ref[...] = roped

    @pl.when(n >= nrope)
    def _():
        kv_ref[...] = acc


def _qkv(h, w, b, cos_t, sin_t, *, tm, tn, tps):
    rows = h.shape[0]
    ncols = w.shape[1]
    nq = Q_COLS // tn
    nrope = (Q_COLS + KV_DUP) // tn
    return pl.pallas_call(
        functools.partial(_qkv_kernel, nq=nq, nrope=nrope),
        grid=(rows // tm, ncols // tn),
        in_specs=[
            pl.BlockSpec((tm, D_MODEL), lambda m, n: (m, 0)),
            pl.BlockSpec((D_MODEL, tn), lambda m, n: (0, n)),
            pl.BlockSpec((1, tn), lambda m, n: (0, n)),
            pl.BlockSpec((tm, LANES), lambda m, n: (m % tps, 0)),
            pl.BlockSpec((tm, LANES), lambda m, n: (m % tps, 0)),
        ],
        out_specs=[
            pl.BlockSpec((tm, tn), lambda m, n: (m, jnp.minimum(n, nq - 1))),
            pl.BlockSpec((tm, tn), lambda m, n: (m, jnp.maximum(n - nq, 0))),
        ],
        out_shape=[
            jax.ShapeDtypeStruct((rows, Q_COLS), BF16),
            jax.ShapeDtypeStruct((rows, 2 * KV_DUP), F32),
        ],
        compiler_params=_params(),
        name="qkv_rope",
    )(h, w, b, cos_t, sin_t)


def _softmax_pair(s, sink, mask):
    keys = s.shape[1] // 2
    ps, ds = [], []
    for i in range(2):
        si = jnp.where(mask, s[:, i * keys:(i + 1) * keys], -jnp.inf)
        sk = sink[:, i * keys:i * keys + 1]
        mx = jnp.maximum(jnp.max(si, axis=-1, keepdims=True), sk)
        pi = jnp.exp(si - mx)
        ps.append(pi)
        ds.append(jnp.sum(pi, axis=-1, keepdims=True) + jnp.exp(sk - mx))
    return jnp.concatenate(ps, axis=1), ds


def _block_diag(x2):
    lane = lax.broadcasted_iota(jnp.int32, x2.shape, 1)
    lo = jnp.where(lane < HEAD_DIM, x2, 0.0)
    hi = jnp.where(lane >= HEAD_DIM, x2, 0.0)
    return jnp.concatenate([lo, hi], axis=0).astype(BF16)


def _attn_kernel(q_ref, kvp_ref, kvc_ref, sink_ref, o_ref, *, blocks_per_seq):
    i = pl.program_id(0)
    no_prev = jnp.where((i % blocks_per_seq) > 0, 0, 2 * WINDOW)
    qi = lax.broadcasted_iota(jnp.int32, (WINDOW, 2 * WINDOW), 0)
    kj = lax.broadcasted_iota(jnp.int32, (WINDOW, 2 * WINDOW), 1)
    mask = ((kj < WINDOW) & (kj > qi + no_prev)) | ((kj >= WINDOW) & (kj - WINDOW <= qi))
    lane = lax.broadcasted_iota(jnp.int32, (WINDOW, LANES), 1)
    pairs = GQA_GROUP // 2
    for kh in range(N_KV_HEADS):
        kc = slice(kh * LANES, (kh + 1) * LANES)
        vc = slice(KV_DUP + kh * LANES, KV_DUP + (kh + 1) * LANES)
        kbd = _block_diag(jnp.concatenate([kvp_ref[:, kc], kvc_ref[:, kc]], axis=0))
        vbd = _block_diag(jnp.concatenate([kvp_ref[:, vc], kvc_ref[:, vc]], axis=0))
        for pi in range(pairs):
            pair = kh * pairs + pi
            qc = slice(pair * LANES, (pair + 1) * LANES)
            s = lax.dot_general(q_ref[:, qc], kbd, (((1,), (1,)), ((), ())),
                                preferred_element_type=F32)
            p, ds = _softmax_pair(s, sink_ref[pair:pair + 1, :], mask)
            o = _dot(p.astype(BF16), vbd)
            o_ref[:, qc] = (o / jnp.where(lane < HEAD_DIM, ds[0], ds[1])).astype(o_ref.dtype)


def _attn_prompt(q, kv, sink_t):
    rows = q.shape[0]
    bps = SEQ // WINDOW
    return pl.pallas_call(
        functools.partial(_attn_kernel, blocks_per_seq=bps),
        grid=(rows // WINDOW,),
        in_specs=[
            pl.BlockSpec((WINDOW, Q_COLS), lambda i: (i, 0)),
            pl.BlockSpec((WINDOW, 2 * KV_DUP), lambda i: (jnp.where(i % bps == 0, i, i - 1), 0)),
            pl.BlockSpec((WINDOW, 2 * KV_DUP), lambda i: (i, 0)),
            pl.BlockSpec(sink_t.shape, lambda i: (0, 0)),
        ],
        out_specs=pl.BlockSpec((WINDOW, Q_COLS), lambda i: (i, 0)),
        out_shape=jax.ShapeDtypeStruct((rows, Q_COLS), BF16),
        compiler_params=_params1(),
        name="attn_banded",
    )(q, kv, kv, sink_t)


def _attn_sample_kernel(q_ref, k_ref, v_ref, sink_ref, o_ref, *, n_keys, kpos0):
    rows = GQA_GROUP * DEC_SEQ
    keys = k_ref.shape[2]
    t = lax.broadcasted_iota(jnp.int32, (rows, keys), 0) % DEC_SEQ
    j = lax.broadcasted_iota(jnp.int32, (rows, keys), 1)
    mask = (j > t) & (j <= t + WINDOW) & (j < n_keys) & (j + kpos0 >= 0)
    for kh in range(N_KV_HEADS):
        s = lax.dot_general(q_ref[0, kh], k_ref[0, kh].astype(BF16), (((1,), (1,)), ((), ())),
                            preferred_element_type=F32)
        s = jnp.where(mask, s, -jnp.inf)
        sk = sink_ref[kh][:, 0:1]
        mx = jnp.maximum(jnp.max(s, axis=-1, keepdims=True), sk)
        p = jnp.exp(s - mx)
        d = jnp.sum(p, axis=-1, keepdims=True) + jnp.exp(sk - mx)
        o = _dot(p.astype(BF16), v_ref[0, kh].astype(BF16))
        o_ref[0, kh] = (o / d).astype(o_ref.dtype)


def _attn_sample(q, k, v, sink_s, n_keys):
    nb, nkv, rows, d = q.shape
    keys = k.shape[2]
    return pl.pallas_call(
        functools.partial(_attn_sample_kernel, n_keys=n_keys, kpos0=PAST_LEN - WINDOW),
        grid=(nb,),
        in_specs=[
            pl.BlockSpec((1, nkv, rows, d), lambda b: (b, 0, 0, 0)),
            pl.BlockSpec((1, nkv, keys, d), lambda b: (b, 0, 0, 0)),
            pl.BlockSpec((1, nkv, keys, d), lambda b: (b, 0, 0, 0)),
            pl.BlockSpec(sink_s.shape, lambda b: (0, 0, 0)),
        ],
        out_specs=pl.BlockSpec((1, nkv, rows, d), lambda b: (b, 0, 0, 0)),
        out_shape=jax.ShapeDtypeStruct(q.shape, BF16),
        compiler_params=_params1(),
        name="attn_sample",
    )(q, k, v, sink_s)


def _ffn_up_kernel(h_ref, wg_ref, wu_ref, cwg_ref, cwu_ref, ctxg_ref, ctxu_ref,
                   act_ref, tailg_ref, tailu_ref, wgb_ref, wub_ref, zg_ref, zu_ref,
                   *, tm, hdr, shift, tps, chunk):
    m = pl.program_id(1)

    @pl.when(m == 0)
    def _():
        wgb_ref[...] = wg_ref[...].astype(BF16)
        wub_ref[...] = wu_ref[...].astype(BF16)

    first = m % tps == 0
    zg_ref[0:hdr, :] = jnp.where(first, ctxg_ref[...], zg_ref[tm:tm + hdr, :])
    zu_ref[0:hdr, :] = jnp.where(first, ctxu_ref[...], zu_ref[tm:tm + hdr, :])
    cwg = cwg_ref[...]
    cwu = cwu_ref[...]

    def conv(z_ref, cw, cur, r0):
        lo = hdr + r0
        y = z_ref[lo - 2 * shift:lo - 2 * shift + chunk, :] * cw[0:1, :]
        y = y + z_ref[lo - shift:lo - shift + chunk, :] * cw[1:2, :]
        return y + cur * cw[2:3, :]

    for r0 in range(0, tm, chunk):
        hc = h_ref[r0:r0 + chunk, :]
        up_g = _dot(hc, wgb_ref[...])
        up_u = _dot(hc, wub_ref[...])
        zg_ref[hdr + r0:hdr + r0 + chunk, :] = up_g
        zu_ref[hdr + r0:hdr + r0 + chunk, :] = up_u
        gate = conv(zg_ref, cwg, up_g, r0)
        lin = conv(zu_ref, cwu, up_u, r0)
        silu = gate / (1.0 + jnp.exp(-gate))
        act_ref[r0:r0 + chunk, :] = (silu * lin).astype(act_ref.dtype)

    tailg_ref[...] = zg_ref[tm:tm + hdr, :]
    tailu_ref[...] = zu_ref[tm:tm + hdr, :]


def _ffn_up(h, w_up, conv_w, ctx, *, tm, tn, hdr, shift, tps, chunk):
    rows = h.shape[0]
    nm, nn = rows // tm, D_FF // tn
    kern = functools.partial(_ffn_up_kernel, tm=tm, hdr=hdr, shift=shift, tps=tps, chunk=chunk)
    lo = lambda n, m: (0, n)
    hi = lambda n, m: (0, n + nn)
    tail_shape = jax.ShapeDtypeStruct((nm * hdr, D_FF), F32)
    return pl.pallas_call(
        kern,
        grid=(nn, nm),
        in_specs=[
            pl.BlockSpec((tm, D_MODEL), lambda n, m: (m, 0)),
            pl.BlockSpec((D_MODEL, tn), lo),
            pl.BlockSpec((D_MODEL, tn), hi),
            pl.BlockSpec((3, tn), lo),
            pl.BlockSpec((3, tn), hi),
            pl.BlockSpec((hdr, tn), lambda n, m: (m // tps, n)),
            pl.BlockSpec((hdr, tn), lambda n, m: (m // tps, n + nn)),
        ],
        out_specs=[
            pl.BlockSpec((tm, tn), lambda n, m: (m, n)),
            pl.BlockSpec((hdr, tn), lambda n, m: (m, n)),
            pl.BlockSpec((hdr, tn), lambda n, m: (m, n)),
        ],
        out_shape=[jax.ShapeDtypeStruct((rows, D_FF), BF16), tail_shape, tail_shape],
        scratch_shapes=[pltpu.VMEM((D_MODEL, tn), BF16), pltpu.VMEM((D_MODEL, tn), BF16),
                        pltpu.VMEM((hdr + tm, tn), F32), pltpu.VMEM((hdr + tm, tn), F32)],
        compiler_params=_params(),
        name="ffn_up",
    )(h, w_up, w_up, conv_w, conv_w, ctx, ctx)


class _Layout:
    def __init__(self, *, prompt):
        if prompt:
            self.tm, self.tps, self.shift, self.pos0 = 1024, SEQ // 1024, 1, 0
            self.tm_down, self.tr, self.chunk = 512, 512, 256
        else:
            self.tm, self.tps, self.shift, self.pos0 = DEC_BATCH * DEC_SEQ, 1, DEC_BATCH, PAST_LEN
            self.tm_down, self.tr, self.chunk = self.tm, self.tm, self.tm
        unit = SUBLANES if prompt else DEC_BATCH
        self.hdr_conv = max(SUBLANES, 2 * unit)
        self.hdr_pool = (POOL_CTX + 1) * (1 if prompt else DEC_BATCH)


def _ffn(x, lay, norm_g, w_up, conv_w, w_down, ctx):
    h = _rmsnorm(x, norm_g, BF16, lay.tr)
    act, tail_g, tail_u = _ffn_up(h, w_up, conv_w, ctx, tm=lay.tm, tn=256, hdr=lay.hdr_conv,
                                  shift=lay.shift, tps=lay.tps, chunk=lay.chunk)
    x = _res_mm(x, [act], w_down, tm=lay.tm_down, tn=512, name="ffn_down")
    return x, jnp.concatenate([tail_g, tail_u], axis=1)


def _trunk(x, lay, ctx_a, ctx_pool, ctx_ffn, attn_fn, wts):
    h = _rmsnorm(x, wts["norm_mix"][0], BF16, lay.tr)
    y_a, tail_a = _mix_a(h, wts["w_in"], wts["conv_a"], ctx_a, tm=lay.tm, tn=256,
                         hdr=lay.hdr_conv, shift=lay.shift, tps=lay.tps)
    y_b, tail_p = _mix_b(h, wts["w_in"], wts["w_pool"], wts["pool_scale"], ctx_pool, tm=lay.tm,
                         hdr=lay.hdr_pool, shift=lay.shift, tps=lay.tps, pos0=lay.pos0)
    x = _res_mm(x, [y_a, y_b], wts["w_out"], tm=lay.tm, tn=512, name="out_proj")
    x, tail_f0 = _ffn(x, lay, wts["norm_ffn"][0], wts["w_up"][0], wts["conv_ffn"][0],
                      wts["w_down"][0], ctx_ffn[0])

    h = _rmsnorm(x, wts["norm_mix"][1], BF16, lay.tr)
    cos_t, sin_t = _rope_table(lay.tm * lay.tps, lay.pos0, lay.shift)
    q, kv = _qkv(h, wts["w_qkv"], wts["b_qkv"], cos_t, sin_t, tm=lay.tm, tn=512, tps=lay.tps)
    o, new_k, new_v = attn_fn(q, kv)
    x = _res_mm(x, [o], wts["w_o"], tm=lay.tm, tn=512, name="attn_out")
    x, tail_f1 = _ffn(x, lay, wts["norm_ffn"][1], wts["w_up"][1], wts["conv_ffn"][1],
                      wts["w_down"][1], ctx_ffn[1])
    y = _rmsnorm(x, wts["norm_final"], F32, lay.tr)
    return y, tail_a, tail_p, new_k, new_v, (tail_f0, tail_f1)


def _dup_heads(w):
    lead = w.shape[:-1]
    w = w.reshape(lead + (N_KV_HEADS, 1, HEAD_DIM))
    return jnp.broadcast_to(w, lead + (N_KV_HEADS, 2, HEAD_DIM)).reshape(lead + (KV_DUP,))


def _undup_heads(kv_half):
    return kv_half.reshape(kv_half.shape[0], N_KV_HEADS, 2, HEAD_DIM)[:, :, 0, :]


def kernel(x_prompt, x_sample, state_conv_a, state_pool, cache_win_k, cache_win_v, state_ffn_conv,
           norm_mix, w_in_ab, conv_a, w_pool, pool_scale, w_out_ab,
           w_qkv, b_qkv, sinks, w_o, norm_ffn, w_up, conv_ffn, w_down, norm_final):
    bp = x_prompt.shape[0]
    nkq = N_KV_HEADS * HEAD_DIM

    wq, wk, wv = (w_qkv[0][:, :Q_COLS], w_qkv[0][:, Q_COLS:Q_COLS + nkq], w_qkv[0][:, Q_COLS + nkq:])
    bq, bk, bv = (b_qkv[0][:Q_COLS], b_qkv[0][Q_COLS:Q_COLS + nkq], b_qkv[0][Q_COLS + nkq:])
    wts = {
        "norm_mix": norm_mix, "norm_ffn": norm_ffn, "norm_final": norm_final,
        "w_in": w_in_ab[0].astype(BF16), "conv_a": conv_a[0],
        "w_pool": w_pool[0].astype(BF16), "pool_scale": pool_scale[0].reshape(1, D_B),
        "w_out": w_out_ab[0].astype(BF16),
        "w_qkv": jnp.concatenate([wq, _dup_heads(wk), _dup_heads(wv)], axis=1).astype(BF16),
        "b_qkv": jnp.concatenate([bq, _dup_heads(bk), _dup_heads(bv)]).reshape(1, -1),
        "w_o": w_o[0].astype(BF16),
        "w_up": [w_up[i] for i in range(2)],
        "conv_ffn": [conv_ffn[i] for i in range(2)],
        "w_down": [w_down[i].astype(BF16) for i in range(2)],
    }
    sink = sinks[0].astype(F32)
    sink_t = jnp.repeat(sink.reshape(N_HEADS // 2, 2), 2 * WINDOW, axis=1)
    sink_s = jnp.broadcast_to(
        jnp.repeat(sink.reshape(N_KV_HEADS, GQA_GROUP), DEC_SEQ, axis=1)[:, :, None],
        (N_KV_HEADS, GQA_GROUP * DEC_SEQ, LANES))

    lay = _Layout(prompt=True)
    rows = bp * SEQ
    tiles = rows // lay.tm
    last_tiles = slice(lay.tps - 1, tiles, lay.tps)

    def attn_prompt(q, kv):
        o = _attn_prompt(q, kv, sink_t)
        kv_last = kv.reshape(bp, SEQ, 2 * KV_DUP)[:, SEQ - WINDOW:].reshape(bp * WINDOW, 2 * KV_DUP)
        new_k = _undup_heads(kv_last[:, :KV_DUP]).reshape(bp, WINDOW, N_KV_HEADS, HEAD_DIM)
        new_v = _undup_heads(kv_last[:, KV_DUP:]).reshape(bp, WINDOW, N_KV_HEADS, HEAD_DIM)
        return o, new_k, new_v

    y_p, tail_a, tail_p, k_p, v_p, tails_f = _trunk(
        x_prompt.reshape(rows, D_MODEL), lay,
        jnp.zeros((bp * lay.hdr_conv, D_A), F32), jnp.zeros((bp * lay.hdr_pool, D_B), F32),
        [jnp.zeros((bp * lay.hdr_conv, 2 * D_FF), F32)] * 2, attn_prompt, wts)

    def last_rows(tail, hdr, keep):
        return tail.reshape(tiles, hdr, -1)[last_tiles, hdr - keep:]

    ca_p = last_rows(tail_a, lay.hdr_conv, 2)[None]
    pl_p = last_rows(tail_p, lay.hdr_pool, POOL_CTX)[None]
    f_p = jnp.stack([last_rows(t, lay.hdr_conv, 2) for t in tails_f])
    y_p = y_p.reshape(bp, SEQ, D_MODEL)

    lay = _Layout(prompt=False)
    nb = x_sample.shape[0]

    def to_tm(a):
        return jnp.swapaxes(a, 0, 1).reshape(-1, a.shape[-1])

    def from_tm(a):
        return jnp.swapaxes(a.reshape(-1, nb, a.shape[-1]), 0, 1)

    def pad_hdr(a, hdr):
        return jnp.concatenate([jnp.zeros((hdr - a.shape[0], a.shape[1]), F32), a], axis=0)

    def attn_sample(q, kv):
        k_new = from_tm(_undup_heads(kv[:, :KV_DUP]).reshape(-1, nkq))
        v_new = from_tm(_undup_heads(kv[:, KV_DUP:]).reshape(-1, nkq))
        kk = jnp.concatenate([cache_win_k[0].reshape(nb, WINDOW, nkq), k_new], axis=1)
        vv = jnp.concatenate([cache_win_v[0].reshape(nb, WINDOW, nkq), v_new], axis=1)
        n_keys = WINDOW + DEC_SEQ
        pad = 2 * WINDOW - n_keys

        def heads_major(a):
            a = jnp.pad(a, ((0, 0), (0, pad), (0, 0)))
            return a.reshape(nb, 2 * WINDOW, N_KV_HEADS, HEAD_DIM).transpose(0, 2, 1, 3)

        qs = from_tm(q).reshape(nb, DEC_SEQ, N_KV_HEADS, GQA_GROUP, HEAD_DIM)
        qs = qs.transpose(0, 2, 3, 1, 4).reshape(nb, N_KV_HEADS, GQA_GROUP * DEC_SEQ, HEAD_DIM)
        o = _attn_sample(qs, heads_major(kk), heads_major(vv), sink_s, n_keys)
        o = o.reshape(nb, N_KV_HEADS, GQA_GROUP, DEC_SEQ, HEAD_DIM).transpose(0, 3, 1, 2, 4)
        o = to_tm(o.reshape(nb, DEC_SEQ, Q_COLS))
        new_k = kk[:, n_keys - WINDOW:].reshape(nb, WINDOW, N_KV_HEADS, HEAD_DIM)
        new_v = vv[:, n_keys - WINDOW:].reshape(nb, WINDOW, N_KV_HEADS, HEAD_DIM)
        return o, new_k, new_v

    y_s, tail_a, tail_p, k_s, v_s, tails_f = _trunk(
        to_tm(x_sample), lay,
        pad_hdr(to_tm(state_conv_a[0]), lay.hdr_conv), pad_hdr(to_tm(state_pool[0]), lay.hdr_pool),
        [pad_hdr(to_tm(state_ffn_conv[i]), lay.hdr_conv) for i in range(2)], attn_sample, wts)

    def last_steps(tail, keep):
        return from_tm(tail[tail.shape[0] - keep * nb:])

    ca_s = last_steps(tail_a, 2)[None]
    pl_s = last_steps(tail_p, POOL_CTX)[None]
    f_s = jnp.stack([last_steps(t, 2) for t in tails_f])
    y_s = from_tm(y_s)

    return (y_p, y_s, ca_p, ca_s, pl_p, pl_s, k_p[None], k_s[None], v_p[None], v_s[None], f_p, f_s)
```

```python
import functools
from typing import NamedTuple

import jax
import jax.numpy as jnp
from jax import lax
from jax.experimental import pallas as pl
from jax.experimental.pallas import tpu as pltpu

D_MODEL = 4096
SEQ = 2048
DEC_BATCH = 32
DEC_SEQ = 4
PAST_LEN = 16384
D_A = D_MODEL // 2
D_B = D_MODEL // 2
POOL_WINDOWS = (2, 4, 8, 16)
POOL_GROUP = D_B // len(POOL_WINDOWS)
POOL_CTX = max(POOL_WINDOWS) - 1
HEAD_DIM = 64
N_HEADS = D_MODEL // HEAD_DIM
N_KV_HEADS = 8
GQA_GROUP = N_HEADS // N_KV_HEADS
WINDOW = 128
ROPE_THETA = 10000.0
D_FF = 11008
EPS = 1e-6

F32 = jnp.float32
BF16 = jnp.bfloat16

LANES = 128
SUBLANES = 8
VMEM_LIMIT = 56 * 1024 * 1024
Q_COLS = N_HEADS * HEAD_DIM
KV_COLS = N_KV_HEADS * HEAD_DIM
KV_DUP = N_KV_HEADS * LANES
TN_WIDE = 512
TN_PAIR = 256


class _Group(NamedTuple):
    rows: int
    tm: int
    tps: int
    shift: int
    pos0: int
    chunk: int
    hdr_conv: int
    hdr_pool: int
    tm_down: int
    tr: int


def _prompt_group(batch):
    tm = 1024
    return _Group(rows=batch * SEQ, tm=tm, tps=SEQ // tm, shift=1, pos0=0, chunk=256,
                  hdr_conv=SUBLANES, hdr_pool=POOL_CTX + 1, tm_down=512, tr=512)


def _sample_group():
    rows = DEC_BATCH * DEC_SEQ
    return _Group(rows=rows, tm=rows, tps=1, shift=DEC_BATCH, pos0=PAST_LEN, chunk=rows,
                  hdr_conv=2 * DEC_BATCH, hdr_pool=(POOL_CTX + 1) * DEC_BATCH, tm_down=rows, tr=rows)


def _params(n_axes=2):
    return pltpu.CompilerParams(
        dimension_semantics=("arbitrary",) * n_axes, vmem_limit_bytes=VMEM_LIMIT)


def _dot(a, b):
    return jnp.dot(a, b, preferred_element_type=F32)


def _div_pow2(x, d):
    assert d & (d - 1) == 0
    return lax.shift_right_logical(x, d.bit_length() - 1)


def _round_weights(pairs):
    for src, dst in pairs:
        dst[...] = src[...].astype(BF16)


def _norm_kernel(x_ref, g_ref, o_ref):
    x = x_ref[...]
    y = x * lax.rsqrt(jnp.mean(x * x, axis=-1, keepdims=True) + EPS)
    o_ref[...] = (y * g_ref[...]).astype(o_ref.dtype)


def _rmsnorm(x, g, out_dtype, tr):
    rows, d = x.shape
    return pl.pallas_call(
        _norm_kernel,
        grid=(rows // tr,),
        in_specs=[pl.BlockSpec((tr, d), lambda i: (i, 0)),
                  pl.BlockSpec((1, d), lambda i: (0, 0))],
        out_specs=pl.BlockSpec((tr, d), lambda i: (i, 0)),
        out_shape=jax.ShapeDtypeStruct((rows, d), out_dtype),
        compiler_params=_params(1),
        name="rmsnorm",
    )(x, g.reshape(1, d))


def _set_header(z_ref, ctx_ref, first, tm, hdr):
    if first is None:
        z_ref[0:hdr, :] = ctx_ref[...]
        return

    @pl.when(first)
    def _():
        z_ref[0:hdr, :] = ctx_ref[...]

    @pl.when(jnp.logical_not(first))
    def _():
        z_ref[0:hdr, :] = z_ref[tm:tm + hdr, :]


def _conv3(z_ref, cw, cur, lo, shift, rows):
    y = z_ref[lo - 2 * shift:lo - 2 * shift + rows, :] * cw[0:1, :]
    y = y + z_ref[lo - shift:lo - shift + rows, :] * cw[1:2, :]
    return y + cur * cw[2:3, :]


def _group_specs(g, hdr, tn, col_off=0):
    ctx = pl.BlockSpec((hdr, tn), lambda n, m: (m // g.tps, n + col_off))
    outs = [pl.BlockSpec((g.tm, tn), lambda n, m: (m, n)),
            pl.BlockSpec((hdr, tn), lambda n, m: (m, n))]
    return ctx, outs


def _single_specs(g, hdr, tn, col_off=0):
    ctx = pl.BlockSpec((hdr, tn), lambda n, m: (0, n + col_off))
    outs = [pl.BlockSpec((g.tm, tn), lambda n, m: (0, n)),
            pl.BlockSpec((hdr, tn), lambda n, m: (0, n))]
    return ctx, outs


def _lhs_specs(gp, gs, k):
    return [pl.BlockSpec((gp.tm, k), lambda n, m: (m, 0)),
            pl.BlockSpec((gs.tm, k), lambda n, m: (0, 0))]


def _mix_a_rows(h_ref, ctx_ref, y_ref, tail_ref, z_ref, wb, wc, wx, cw, *, g, first):
    tm, hdr = g.tm, g.hdr_conv
    _set_header(z_ref, ctx_ref, first, tm, hdr)
    for r0 in range(0, tm, g.chunk):
        hc = h_ref[r0:r0 + g.chunk, :]
        gate_b = _dot(hc, wb[...])
        u = _dot(hc, wc[...]) * _dot(hc, wx[...])
        z_ref[hdr + r0:hdr + r0 + g.chunk, :] = u
        conv = _conv3(z_ref, cw, u, hdr + r0, g.shift, g.chunk)
        y_ref[r0:r0 + g.chunk, :] = (gate_b * conv).astype(y_ref.dtype)
    tail_ref[...] = z_ref[tm:tm + hdr, :]


def _mix_a_kernel(hp_ref, hs_ref, wb_ref, wc_ref, wx_ref, cw_ref, ctxp_ref, ctxs_ref,
                  yp_ref, tailp_ref, ys_ref, tails_ref,
                  wbb_ref, wcb_ref, wxb_ref, zp_ref, zs_ref, *, gp, gs):
    m = pl.program_id(1)
    cw = cw_ref[...]
    rows = functools.partial(_mix_a_rows, wb=wbb_ref, wc=wcb_ref, wx=wxb_ref, cw=cw)

    @pl.when(m == 0)
    def _():
        _round_weights([(wb_ref, wbb_ref), (wc_ref, wcb_ref), (wx_ref, wxb_ref)])
        rows(hs_ref, ctxs_ref, ys_ref, tails_ref, zs_ref, g=gs, first=None)

    rows(hp_ref, ctxp_ref, yp_ref, tailp_ref, zp_ref, g=gp, first=m % gp.tps == 0)


def _mix_a(hp, hs, w_in, conv_w, ctxp, ctxs, gp, gs):
    tn = TN_PAIR
    nn, nm = D_A // tn, gp.rows // gp.tm
    ctxp_spec, outp = _group_specs(gp, gp.hdr_conv, tn)
    ctxs_spec, outs = _single_specs(gs, gs.hdr_conv, tn)
    wspec = lambda off: pl.BlockSpec((D_MODEL, tn), lambda n, m: (0, n + off))
    return pl.pallas_call(
        functools.partial(_mix_a_kernel, gp=gp, gs=gs),
        grid=(nn, nm),
        in_specs=_lhs_specs(gp, gs, D_MODEL) + [
            wspec(0), wspec(nn), wspec(2 * nn),
            pl.BlockSpec((3, tn), lambda n, m: (0, n)),
            ctxp_spec, ctxs_spec],
        out_specs=outp + outs,
        out_shape=[
            jax.ShapeDtypeStruct((gp.rows, D_A), BF16),
            jax.ShapeDtypeStruct((nm * gp.hdr_conv, D_A), F32),
            jax.ShapeDtypeStruct((gs.rows, D_A), BF16),
            jax.ShapeDtypeStruct((gs.hdr_conv, D_A), F32),
        ],
        scratch_shapes=[pltpu.VMEM((D_MODEL, tn), BF16)] * 3 + [
            pltpu.VMEM((gp.hdr_conv + gp.tm, tn), F32), pltpu.VMEM((gs.hdr_conv + gs.tm, tn), F32)],
        compiler_params=_params(),
        name="mix_a",
    )(hp, hs, w_in, w_in, w_in, conv_w, ctxp, ctxs)


def _mix_b_rows(h_ref, ctx_ref, y_ref, tail_ref, z_ref, wp, wg, scale, *, g, first, tile, win):
    tm, hdr = g.tm, g.hdr_pool
    _set_header(z_ref, ctx_ref, first, tm, hdr)
    for r0 in range(0, tm, g.chunk):
        p = _dot(h_ref[r0:r0 + g.chunk, :], wp[...])
        lo = hdr + r0
        z_ref[lo:lo + g.chunk, :] = p
        acc = p
        for i in range(1, win):
            acc = acc + z_ref[lo - i * g.shift:lo - i * g.shift + g.chunk, :]
        row = lax.broadcasted_iota(jnp.int32, (g.chunk, 1), 0) + (tile * tm + r0)
        pos = g.pos0 + _div_pow2(row, g.shift)
        cnt = jnp.minimum(pos + 1, win).astype(F32)
        pooled = acc / cnt - p
        yb = _dot(pooled.astype(BF16), wg[...]) * scale
        y_ref[r0:r0 + g.chunk, :] = yb.astype(y_ref.dtype)
    tail_ref[...] = z_ref[tm:tm + hdr, :]


def _mix_b_kernel(hp_ref, hs_ref, wp_ref, wg_ref, sc_ref, ctxp_ref, ctxs_ref,
                  yp_ref, tailp_ref, ys_ref, tails_ref,
                  wpb_ref, wgb_ref, zp_ref, zs_ref, *, gp, gs):
    grp = pl.program_id(0)
    m = pl.program_id(1)
    scale = sc_ref[...]

    @pl.when(m == 0)
    def _():
        _round_weights([(wp_ref, wpb_ref), (wg_ref, wgb_ref)])

    for gi, win in enumerate(POOL_WINDOWS):
        rows = functools.partial(_mix_b_rows, wp=wpb_ref, wg=wgb_ref, scale=scale, win=win)

        @pl.when(jnp.logical_and(grp == gi, m == 0))
        def _(rows=rows):
            rows(hs_ref, ctxs_ref, ys_ref, tails_ref, zs_ref, g=gs, first=None, tile=0)

        @pl.when(grp == gi)
        def _(rows=rows):
            rows(hp_ref, ctxp_ref, yp_ref, tailp_ref, zp_ref, g=gp,
                 first=m % gp.tps == 0, tile=m % gp.tps)


def _mix_b(hp, hs, w_in, w_grp, scale, ctxp, ctxs, gp, gs):
    tn, ng = POOL_GROUP, len(POOL_WINDOWS)
    nm = gp.rows // gp.tm
    col0 = 3 * D_A // tn
    ctxp_spec, outp = _group_specs(gp, gp.hdr_pool, tn)
    ctxs_spec, outs = _single_specs(gs, gs.hdr_pool, tn)
    return pl.pallas_call(
        functools.partial(_mix_b_kernel, gp=gp, gs=gs),
        grid=(ng, nm),
        in_specs=_lhs_specs(gp, gs, D_MODEL) + [
            pl.BlockSpec((D_MODEL, tn), lambda n, m: (0, col0 + n)),
            pl.BlockSpec((None, tn, tn), lambda n, m: (n, 0, 0)),
            pl.BlockSpec((1, tn), lambda n, m: (0, n)),
            ctxp_spec, ctxs_spec],
        out_specs=outp + outs,
        out_shape=[
            jax.ShapeDtypeStruct((gp.rows, D_B), BF16),
            jax.ShapeDtypeStruct((nm * gp.hdr_pool, D_B), F32),
            jax.ShapeDtypeStruct((gs.rows, D_B), BF16),
            jax.ShapeDtypeStruct((gs.hdr_pool, D_B), F32),
        ],
        scratch_shapes=[
            pltpu.VMEM((D_MODEL, tn), BF16), pltpu.VMEM((tn, tn), BF16),
            pltpu.VMEM((gp.hdr_pool + gp.tm, tn), F32), pltpu.VMEM((gs.hdr_pool + gs.tm, tn), F32)],
        compiler_params=_params(),
        name="mix_b",
    )(hp, hs, w_in, w_grp, scale, ctxp, ctxs)


def _res_mm_kernel(*refs, n_lhs):
    m = pl.program_id(1)
    xp_ref, xs_ref = refs[0], refs[1]
    ap = refs[2:2 + n_lhs]
    a_s = refs[2 + n_lhs:2 + 2 * n_lhs]
    w = refs[2 + 2 * n_lhs:2 + 3 * n_lhs]
    op_ref, os_ref = refs[2 + 3 * n_lhs], refs[3 + 3 * n_lhs]
    wb = refs[4 + 3 * n_lhs:]

    def rows(x_ref, a_refs, o_ref):
        acc = x_ref[...]
        for a_ref, wb_ref in zip(a_refs, wb):
            acc = acc + _dot(a_ref[...], wb_ref[...])
        o_ref[...] = acc

    @pl.when(m == 0)
    def _():
        _round_weights(list(zip(w, wb)))
        rows(xs_ref, a_s, os_ref)

    rows(xp_ref, ap, op_ref)


def _res_mm(xp, xs, lhs_p, lhs_s, w, gp, gs, name):
    d_out = xp.shape[1]
    n_lhs = len(lhs_p)
    kdim = lhs_p[0].shape[1]
    tn = TN_WIDE
    xspec = [pl.BlockSpec((gp.tm, tn), lambda n, m: (m, n)),
             pl.BlockSpec((gs.tm, tn), lambda n, m: (0, n))]
    lhs_specs = _lhs_specs(gp, gs, kdim)
    in_specs = xspec + [lhs_specs[0]] * n_lhs + [lhs_specs[1]] * n_lhs
    in_specs += [pl.BlockSpec((kdim, tn), lambda n, m, i=i: (i, n)) for i in range(n_lhs)]
    return pl.pallas_call(
        functools.partial(_res_mm_kernel, n_lhs=n_lhs),
        grid=(d_out // tn, gp.rows // gp.tm),
        in_specs=in_specs,
        out_specs=xspec,
        out_shape=[jax.ShapeDtypeStruct(xp.shape, F32), jax.ShapeDtypeStruct(xs.shape, F32)],
        scratch_shapes=[pltpu.VMEM((kdim, tn), BF16)] * n_lhs,
        compiler_params=_params(),
        name=name,
    )(xp, xs, *lhs_p, *lhs_s, *([w] * n_lhs))


def _down_kernel(x_ref, a_ref, w_ref, o_ref):
    o_ref[...] = x_ref[...] + _dot(a_ref[...], w_ref[...])


def _ffn_down(x, act, w, *, tm):
    rows, d_out = x.shape
    kdim = act.shape[1]
    tn = TN_WIDE
    return pl.pallas_call(
        _down_kernel,
        grid=(rows // tm, d_out // tn),
        in_specs=[pl.BlockSpec((tm, tn), lambda m, n: (m, n)),
                  pl.BlockSpec((tm, kdim), lambda m, n: (m, 0)),
                  pl.BlockSpec((kdim, tn), lambda m, n: (0, n))],
        out_specs=pl.BlockSpec((tm, tn), lambda m, n: (m, n)),
        out_shape=jax.ShapeDtypeStruct((rows, d_out), F32),
        compiler_params=_params(),
        name="ffn_down",
    )(x, act, w)


def _rope_table_kernel(inv_ref, cos_ref, sin_ref, *, pos0, shift):
    rows = cos_ref.shape[0]
    row = lax.broadcasted_iota(jnp.int32, (rows, LANES), 0)
    lane = lax.broadcasted_iota(jnp.int32, (rows, LANES), 1)
    pos = (pos0 + _div_pow2(row, shift)).astype(F32)
    ang = pos * inv_ref[...]
    first_half = (lane & (HEAD_DIM - 1)) < (HEAD_DIM // 2)
    cos_ref[...] = jnp.cos(ang)
    sin = jnp.sin(ang)
    sin_ref[...] = jnp.where(first_half, -sin, sin)


def _rope_table(rows, pos0, shift):
    half = HEAD_DIM // 2
    inv = ROPE_THETA ** (-jnp.arange(half, dtype=F32) / half)
    inv = jnp.tile(inv, LANES // half).reshape(1, LANES)
    return pl.pallas_call(
        functools.partial(_rope_table_kernel, pos0=pos0, shift=shift),
        out_shape=[jax.ShapeDtypeStruct((rows, LANES), F32)] * 2,
        name="rope_table",
    )(inv)


def _rope(x, cos_ref, sin_ref, r0):
    rows, tn = x.shape
    reps = tn // LANES
    cos = jnp.concatenate([cos_ref[r0:r0 + rows, :]] * reps, axis=1)
    sin = jnp.concatenate([sin_ref[r0:r0 + rows, :]] * reps, axis=1)
    lane = lax.broadcasted_iota(jnp.int32, x.shape, 1)
    half = HEAD_DIM // 2
    first_half = (lane & (HEAD_DIM - 1)) < half
    partner = jnp.where(first_half, pltpu.roll(x, tn - half, 1), pltpu.roll(x, half, 1))
    return x * cos + partner * sin


def _dup_heads(x):
    lane = lax.broadcasted_iota(jnp.int32, (x.shape[0], LANES), 1)
    lower = lane < HEAD_DIM
    out = []
    for c in range(x.shape[1] // LANES):
        v = x[:, c * LANES:(c + 1) * LANES]
        r = pltpu.roll(v, HEAD_DIM, 1)
        out += [jnp.where(lower, v, r), jnp.where(lower, r, v)]
    return jnp.concatenate(out, axis=1)


def _q_rows(h_ref, cos_ref, sin_ref, q_ref, wq, bias, *, g):
    for r0 in range(0, g.tm, g.chunk):
        acc = _dot(h_ref[r0:r0 + g.chunk, :], wq[...]) + bias
        roped = _rope(acc, cos_ref, sin_ref, r0)
        q_ref[r0:r0 + g.chunk, :] = (roped * (HEAD_DIM ** -0.5)).astype(q_ref.dtype)


def _q_kernel(hp_ref, hs_ref, w_ref, b_ref, cosp_ref, sinp_ref, coss_ref, sins_ref,
              qp_ref, qs_ref, wb_ref, *, gp, gs):
    m = pl.program_id(1)
    bias = b_ref[...]

    @pl.when(m == 0)
    def _():
        _round_weights([(w_ref, wb_ref)])
        _q_rows(hs_ref, coss_ref, sins_ref, qs_ref, wb_ref, bias, g=gs)

    _q_rows(hp_ref, cosp_ref, sinp_ref, qp_ref, wb_ref, bias, g=gp)


def _kv_rows(h_ref, cos_ref, sin_ref, kv_ref, wkv, bias, is_key, *, g):
    for r0 in range(0, g.tm, g.chunk):
        acc = _dot(h_ref[r0:r0 + g.chunk, :], wkv[...]) + bias
        roped = _rope(acc, cos_ref, sin_ref, r0)
        kv_ref[r0:r0 + g.chunk, :] = _dup_heads(jnp.where(is_key, roped, acc))


def _kv_kernel(hp_ref, hs_ref, w_ref, b_ref, cosp_ref, sinp_ref, coss_ref, sins_ref,
               kvp_ref, kvs_ref, wb_ref, *, gp, gs):
    is_key = pl.program_id(0) == 0
    m = pl.program_id(1)
    bias = b_ref[...]

    @pl.when(m == 0)
    def _():
        _round_weights([(w_ref, wb_ref)])
        _kv_rows(hs_ref, coss_ref, sins_ref, kvs_ref, wb_ref, bias, is_key, g=gs)

    _kv_rows(hp_ref, cosp_ref, sinp_ref, kvp_ref, wb_ref, bias, is_key, g=gp)


def _qkv(hp, hs, w_qkv, b_qkv, tabp, tabs, gp, gs):
    tn = TN_WIDE
    nm = gp.rows // gp.tm
    tab_specs = [pl.BlockSpec((gp.tm, LANES), lambda n, m: (m % gp.tps, 0))] * 2
    tab_specs += [pl.BlockSpec((gs.tm, LANES), lambda n, m: (0, 0))] * 2

    def call(kern, n_tiles, col_off, out_tn, out_cols, dtype, name):
        return pl.pallas_call(
            functools.partial(kern, gp=gp, gs=gs),
            grid=(n_tiles, nm),
            in_specs=_lhs_specs(gp, gs, D_MODEL) + [
                pl.BlockSpec((D_MODEL, tn), lambda n, m: (0, n + col_off)),
                pl.BlockSpec((1, tn), lambda n, m: (0, n + col_off))] + tab_specs,
            out_specs=[pl.BlockSpec((gp.tm, out_tn), lambda n, m: (m, n)),
                       pl.BlockSpec((gs.tm, out_tn), lambda n, m: (0, n))],
            out_shape=[jax.ShapeDtypeStruct((gp.rows, out_cols), dtype),
                       jax.ShapeDtypeStruct((gs.rows, out_cols), dtype)],
            scratch_shapes=[pltpu.VMEM((D_MODEL, tn), BF16)],
            compiler_params=_params(),
            name=name,
        )(hp, hs, w_qkv, b_qkv, *tabp, *tabs)

    assert KV_COLS == tn
    qp, qs = call(_q_kernel, Q_COLS // tn, 0, tn, Q_COLS, BF16, "q_rope")
    kvp, kvs = call(_kv_kernel, 2, Q_COLS // tn, KV_DUP, 2 * KV_DUP, F32, "kv_rope")
    return qp, qs, kvp, kvs


def _softmax_pair(s, sink, mask):
    keys = s.shape[1] // 2
    ps, ds = [], []
    for i in range(2):
        si = jnp.where(mask, s[:, i * keys:(i + 1) * keys], -jnp.inf)
        sk = sink[:, i * keys:i * keys + 1]
        mx = jnp.maximum(jnp.max(si, axis=-1, keepdims=True), sk)
        pi = jnp.exp(si - mx)
        ps.append(pi)
        ds.append(jnp.sum(pi, axis=-1, keepdims=True) + jnp.exp(sk - mx))
    return jnp.concatenate(ps, axis=1), ds


def _block_diag(x2):
    lane = lax.broadcasted_iota(jnp.int32, x2.shape, 1)
    lo = jnp.where(lane < HEAD_DIM, x2, 0.0)
    hi = jnp.where(lane >= HEAD_DIM, x2, 0.0)
    return jnp.concatenate([lo, hi], axis=0).astype(BF16)


def _attn_kernel(q_ref, kvp_ref, kvc_ref, sink_ref, o_ref, *, blocks_per_seq):
    i = pl.program_id(0)
    no_prev = jnp.where((i % blocks_per_seq) > 0, 0, 2 * WINDOW)
    qi = lax.broadcasted_iota(jnp.int32, (WINDOW, 2 * WINDOW), 0)
    kj = lax.broadcasted_iota(jnp.int32, (WINDOW, 2 * WINDOW), 1)
    mask = ((kj < WINDOW) & (kj > qi + no_prev)) | ((kj >= WINDOW) & (kj - WINDOW <= qi))
    lane = lax.broadcasted_iota(jnp.int32, (WINDOW, LANES), 1)
    pairs = GQA_GROUP // 2
    for kh in range(N_KV_HEADS):
        kc = slice(kh * LANES, (kh + 1) * LANES)
        vc = slice(KV_DUP + kh * LANES, KV_DUP + (kh + 1) * LANES)
        kbd = _block_diag(jnp.concatenate([kvp_ref[:, kc], kvc_ref[:, kc]], axis=0))
        vbd = _block_diag(jnp.concatenate([kvp_ref[:, vc], kvc_ref[:, vc]], axis=0))
        for pi in range(pairs):
            pair = kh * pairs + pi
            qc = slice(pair * LANES, (pair + 1) * LANES)
            s = lax.dot_general(q_ref[:, qc], kbd, (((1,), (1,)), ((), ())),
                                preferred_element_type=F32)
            p, ds = _softmax_pair(s, sink_ref[pair:pair + 1, :], mask)
            o = _dot(p.astype(BF16), vbd)
            o_ref[:, qc] = (o / jnp.where(lane < HEAD_DIM, ds[0], ds[1])).astype(o_ref.dtype)


def _attn_prompt(q, kv, sink_t):
    rows = q.shape[0]
    bps = SEQ // WINDOW
    return pl.pallas_call(
        functools.partial(_attn_kernel, blocks_per_seq=bps),
        grid=(rows // WINDOW,),
        in_specs=[
            pl.BlockSpec((WINDOW, Q_COLS), lambda i: (i, 0)),
            pl.BlockSpec((WINDOW, 2 * KV_DUP), lambda i: (jnp.where(i % bps == 0, i, i - 1), 0)),
            pl.BlockSpec((WINDOW, 2 * KV_DUP), lambda i: (i, 0)),
            pl.BlockSpec(sink_t.shape, lambda i: (0, 0)),
        ],
        out_specs=pl.BlockSpec((WINDOW, Q_COLS), lambda i: (i, 0)),
        out_shape=jax.ShapeDtypeStruct((rows, Q_COLS), BF16),
        compiler_params=_params(1),
        name="attn_banded",
    )(q, kv, kv, sink_t)


def _attn_sample_kernel(q_ref, k_ref, v_ref, sink_ref, o_ref, *, n_keys, kpos0):
    rows = GQA_GROUP * DEC_SEQ
    keys = k_ref.shape[2]
    t = lax.broadcasted_iota(jnp.int32, (rows, keys), 0) % DEC_SEQ
    j = lax.broadcasted_iota(jnp.int32, (rows, keys), 1)
    mask = (j > t) & (j <= t + WINDOW) & (j < n_keys) & (j + kpos0 >= 0)
    for kh in range(N_KV_HEADS):
        s = lax.dot_general(q_ref[0, kh], k_ref[0, kh].astype(BF16), (((1,), (1,)), ((), ())),
                            preferred_element_type=F32)
        s = jnp.where(mask, s, -jnp.inf)
        sk = sink_ref[kh][:, 0:1]
        mx = jnp.maximum(jnp.max(s, axis=-1, keepdims=True), sk)
        p = jnp.exp(s - mx)
        d = jnp.sum(p, axis=-1, keepdims=True) + jnp.exp(sk - mx)
        o = _dot(p.astype(BF16), v_ref[0, kh].astype(BF16))
        o_ref[0, kh] = (o / d).astype(o_ref.dtype)


def _attn_sample(q, k, v, sink_s, n_keys):
    nb, nkv, rows, d = q.shape
    keys = k.shape[2]
    return pl.pallas_call(
        functools.partial(_attn_sample_kernel, n_keys=n_keys, kpos0=PAST_LEN - WINDOW),
        grid=(nb,),
        in_specs=[
            pl.BlockSpec((1, nkv, rows, d), lambda b: (b, 0, 0, 0)),
            pl.BlockSpec((1, nkv, keys, d), lambda b: (b, 0, 0, 0)),
            pl.BlockSpec((1, nkv, keys, d), lambda b: (b, 0, 0, 0)),
            pl.BlockSpec(sink_s.shape, lambda b: (0, 0, 0)),
        ],
        out_specs=pl.BlockSpec((1, nkv, rows, d), lambda b: (b, 0, 0, 0)),
        out_shape=jax.ShapeDtypeStruct(q.shape, BF16),
        compiler_params=_params(1),
        name="attn_sample",
    )(q, k, v, sink_s)


def _ffn_up_rows(h_ref, ctxg_ref, ctxu_ref, act_ref, tailg_ref, tailu_ref, zg_ref, zu_ref,
                 wg, wu, cwg, cwu, *, g, first):
    tm, hdr = g.tm, g.hdr_conv
    _set_header(zg_ref, ctxg_ref, first, tm, hdr)
    _set_header(zu_ref, ctxu_ref, first, tm, hdr)
    for r0 in range(0, tm, g.chunk):
        hc = h_ref[r0:r0 + g.chunk, :]
        up_g = _dot(hc, wg[...])
        up_u = _dot(hc, wu[...])
        lo = hdr + r0
        zg_ref[lo:lo + g.chunk, :] = up_g
        zu_ref[lo:lo + g.chunk, :] = up_u
        gate = _conv3(zg_ref, cwg, up_g, lo, g.shift, g.chunk)
        lin = _conv3(zu_ref, cwu, up_u, lo, g.shift, g.chunk)
        silu = gate / (1.0 + jnp.exp(-gate))
        act_ref[r0:r0 + g.chunk, :] = (silu * lin).astype(act_ref.dtype)
    tailg_ref[...] = zg_ref[tm:tm + hdr, :]
    tailu_ref[...] = zu_ref[tm:tm + hdr, :]


def _ffn_up_kernel(hp_ref, hs_ref, wg_ref, wu_ref, cwg_ref, cwu_ref,
                   ctxgp_ref, ctxup_ref, ctxgs_ref, ctxus_ref,
                   actp_ref, tailgp_ref, tailup_ref, acts_ref, tailgs_ref, tailus_ref,
                   wgb_ref, wub_ref, zgp_ref, zup_ref, zgs_ref, zus_ref, *, gp, gs):
    m = pl.program_id(1)
    rows = functools.partial(_ffn_up_rows, wg=wgb_ref, wu=wub_ref, cwg=cwg_ref[...], cwu=cwu_ref[...])

    @pl.when(m == 0)
    def _():
        _round_weights([(wg_ref, wgb_ref), (wu_ref, wub_ref)])
        rows(hs_ref, ctxgs_ref, ctxus_ref, acts_ref, tailgs_ref, tailus_ref, zgs_ref, zus_ref,
             g=gs, first=None)

    rows(hp_ref, ctxgp_ref, ctxup_ref, actp_ref, tailgp_ref, tailup_ref, zgp_ref, zup_ref,
         g=gp, first=m % gp.tps == 0)


def _ffn_up(hp, hs, w_up, conv_w, ctxp, ctxs, gp, gs):
    tn = TN_PAIR
    nn, nm = D_FF // tn, gp.rows // gp.tm
    ctxgp, outp = _group_specs(gp, gp.hdr_conv, tn)
    ctxup, _ = _group_specs(gp, gp.hdr_conv, tn, nn)
    ctxgs, outs = _single_specs(gs, gs.hdr_conv, tn)
    ctxus, _ = _single_specs(gs, gs.hdr_conv, tn, nn)
    lo = lambda n, m: (0, n)
    hi = lambda n, m: (0, n + nn)
    return pl.pallas_call(
        functools.partial(_ffn_up_kernel, gp=gp, gs=gs),
        grid=(nn, nm),
        in_specs=_lhs_specs(gp, gs, D_MODEL) + [
            pl.BlockSpec((D_MODEL, tn), lo), pl.BlockSpec((D_MODEL, tn), hi),
            pl.BlockSpec((3, tn), lo), pl.BlockSpec((3, tn), hi),
            ctxgp, ctxup, ctxgs, ctxus],
        out_specs=outp + outp[1:] + outs + outs[1:],
        out_shape=[
            jax.ShapeDtypeStruct((gp.rows, D_FF), BF16),
            jax.ShapeDtypeStruct((nm * gp.hdr_conv, D_FF), F32),
            jax.ShapeDtypeStruct((nm * gp.hdr_conv, D_FF), F32),
            jax.ShapeDtypeStruct((gs.rows, D_FF), BF16),
            jax.ShapeDtypeStruct((gs.hdr_conv, D_FF), F32),
            jax.ShapeDtypeStruct((gs.hdr_conv, D_FF), F32),
        ],
        scratch_shapes=[pltpu.VMEM((D_MODEL, tn), BF16)] * 2
        + [pltpu.VMEM((gp.hdr_conv + gp.tm, tn), F32)] * 2
        + [pltpu.VMEM((gs.hdr_conv + gs.tm, tn), F32)] * 2,
        compiler_params=_params(),
        name="ffn_up",
    )(hp, hs, w_up, w_up, conv_w, conv_w, ctxp, ctxp, ctxs, ctxs)


def _norm2(xp, xs, g, gp, gs, dtype=BF16):
    return _rmsnorm(xp, g, dtype, gp.tr), _rmsnorm(xs, g, dtype, gs.tr)


def _ffn(xp, xs, gp, gs, norm_g, w_up, conv_w, w_down, ctxp, ctxs):
    hp, hs = _norm2(xp, xs, norm_g, gp, gs)
    actp, tgp, tup, acts, tgs, tus = _ffn_up(hp, hs, w_up, conv_w, ctxp, ctxs, gp, gs)
    w_down = w_down.astype(BF16)
    xp = _ffn_down(xp, actp, w_down, tm=gp.tm_down)
    xs = _ffn_down(xs, acts, w_down, tm=gs.tm_down)
    return xp, xs, jnp.concatenate([tgp, tup], axis=1), jnp.concatenate([tgs, tus], axis=1)


def _undup_heads(kv_half):
    rows = kv_half.shape[0]
    return kv_half.reshape(rows, N_KV_HEADS, 2, HEAD_DIM)[:, :, 0, :].reshape(rows, KV_COLS)


def kernel(x_prompt, x_sample, state_conv_a, state_pool, cache_win_k, cache_win_v, state_ffn_conv,
           norm_mix, w_in_ab, conv_a, w_pool, pool_scale, w_out_ab,
           w_qkv, b_qkv, sinks, w_o, norm_ffn, w_up, conv_ffn, w_down, norm_final):
    bp = x_prompt.shape[0]
    nb = x_sample.shape[0]
    gp, gs = _prompt_group(bp), _sample_group()
    tiles = gp.rows // gp.tm
    last_tiles = slice(gp.tps - 1, tiles, gp.tps)

    def to_tm(a):
        return jnp.swapaxes(a, 0, 1).reshape(-1, a.shape[-1])

    def from_tm(a):
        return jnp.swapaxes(a.reshape(-1, nb, a.shape[-1]), 0, 1)

    def pad_hdr(a, hdr):
        return jnp.concatenate([jnp.zeros((hdr - a.shape[0], a.shape[1]), F32), a], axis=0)

    def last_rows(tail, hdr, keep):
        return tail.reshape(tiles, hdr, -1)[last_tiles, hdr - keep:]

    def last_steps(tail, keep):
        return from_tm(tail[tail.shape[0] - keep * nb:])

    sink = sinks[0].astype(F32)
    sink_t = jnp.repeat(sink.reshape(N_HEADS // 2, 2), 2 * WINDOW, axis=1)
    sink_s = jnp.broadcast_to(
        jnp.repeat(sink.reshape(N_KV_HEADS, GQA_GROUP), DEC_SEQ, axis=1)[:, :, None],
        (N_KV_HEADS, GQA_GROUP * DEC_SEQ, LANES))

    xp = x_prompt.reshape(gp.rows, D_MODEL)
    xs = to_tm(x_sample)
    zeros_conv = lambda c: jnp.zeros((bp * gp.hdr_conv, c), F32)

    hp, hs = _norm2(xp, xs, norm_mix[0], gp, gs)
    yap, tail_ap, yas, tail_as = _mix_a(
        hp, hs, w_in_ab[0], conv_a[0], zeros_conv(D_A), to_tm(state_conv_a[0]), gp, gs)
    ybp, tail_pp, ybs, tail_ps = _mix_b(
        hp, hs, w_in_ab[0], w_pool[0], pool_scale[0].reshape(1, D_B),
        jnp.zeros((bp * gp.hdr_pool, D_B), F32), pad_hdr(to_tm(state_pool[0]), gs.hdr_pool), gp, gs)
    xp, xs = _res_mm(xp, xs, [yap, ybp], [yas, ybs], w_out_ab[0], gp, gs, "out_proj")
    xp, xs, tail_f0p, tail_f0s = _ffn(xp, xs, gp, gs, norm_ffn[0], w_up[0], conv_ffn[0], w_down[0],
                                      zeros_conv(2 * D_FF), to_tm(state_ffn_conv[0]))

    hp, hs = _norm2(xp, xs, norm_mix[1], gp, gs)
    tabp = _rope_table(gp.tm * gp.tps, gp.pos0, gp.shift)
    tabs = _rope_table(gs.tm, gs.pos0, gs.shift)
    qp, qs, kvp, kvs = _qkv(hp, hs, w_qkv[0], b_qkv[0].reshape(1, -1), tabp, tabs, gp, gs)

    op = _attn_prompt(qp, kvp, sink_t)
    kv_last = kvp.reshape(bp, SEQ, 2 * KV_DUP)[:, SEQ - WINDOW:].reshape(bp * WINDOW, 2 * KV_DUP)
    k_p = _undup_heads(kv_last[:, :KV_DUP]).reshape(bp, WINDOW, N_KV_HEADS, HEAD_DIM)
    v_p = _undup_heads(kv_last[:, KV_DUP:]).reshape(bp, WINDOW, N_KV_HEADS, HEAD_DIM)

    n_keys = WINDOW + DEC_SEQ
    kk = jnp.concatenate([cache_win_k[0].reshape(nb, WINDOW, KV_COLS),
                          from_tm(_undup_heads(kvs[:, :KV_DUP]))], axis=1)
    vv = jnp.concatenate([cache_win_v[0].reshape(nb, WINDOW, KV_COLS),
                          from_tm(_undup_heads(kvs[:, KV_DUP:]))], axis=1)

    def heads_major(a):
        a = jnp.pad(a, ((0, 0), (0, 2 * WINDOW - n_keys), (0, 0)))
        return a.reshape(nb, 2 * WINDOW, N_KV_HEADS, HEAD_DIM).transpose(0, 2, 1, 3)

    q5 = from_tm(qs).reshape(nb, DEC_SEQ, N_KV_HEADS, GQA_GROUP, HEAD_DIM)
    q5 = q5.transpose(0, 2, 3, 1, 4).reshape(nb, N_KV_HEADS, GQA_GROUP * DEC_SEQ, HEAD_DIM)
    o5 = _attn_sample(q5, heads_major(kk), heads_major(vv), sink_s, n_keys)
    o5 = o5.reshape(nb, N_KV_HEADS, GQA_GROUP, DEC_SEQ, HEAD_DIM).transpose(0, 3, 1, 2, 4)
    os_ = to_tm(o5.reshape(nb, DEC_SEQ, Q_COLS))
    k_s = kk[:, n_keys - WINDOW:].reshape(nb, WINDOW, N_KV_HEADS, HEAD_DIM)
    v_s = vv[:, n_keys - WINDOW:].reshape(nb, WINDOW, N_KV_HEADS, HEAD_DIM)

    xp, xs = _res_mm(xp, xs, [op], [os_], w_o[0], gp, gs, "attn_out")
    xp, xs, tail_f1p, tail_f1s = _ffn(xp, xs, gp, gs, norm_ffn[1], w_up[1], conv_ffn[1], w_down[1],
                                      zeros_conv(2 * D_FF), to_tm(state_ffn_conv[1]))
    yp, ys = _norm2(xp, xs, norm_final, gp, gs, F32)

    ca_p = last_rows(tail_ap, gp.hdr_conv, 2)[None]
    pl_p = last_rows(tail_pp, gp.hdr_pool, POOL_CTX)[None]
    f_p = jnp.stack([last_rows(t, gp.hdr_conv, 2) for t in (tail_f0p, tail_f1p)])
    ca_s = last_steps(tail_as, 2)[None]
    pl_s = last_steps(tail_ps, POOL_CTX)[None]
    f_s = jnp.stack([last_steps(t, 2) for t in (tail_f0s, tail_f1s)])
    return (yp.reshape(bp, SEQ, D_MODEL), from_tm(ys), ca_p, ca_s, pl_p, pl_s,
            k_p[None], k_s[None], v_p[None], v_s[None], f_p, f_s)
```

```python
import functools
from typing import NamedTuple

import jax
import jax.numpy as jnp
from jax import lax
from jax.experimental import pallas as pl
from jax.experimental.pallas import tpu as pltpu

D_MODEL = 4096
SEQ = 2048
DEC_BATCH = 32
DEC_SEQ = 4
PAST_LEN = 16384
D_A = D_MODEL // 2
D_B = D_MODEL // 2
POOL_WINDOWS = (2, 4, 8, 16)
POOL_GROUP = D_B // len(POOL_WINDOWS)
POOL_CTX = max(POOL_WINDOWS) - 1
HEAD_DIM = 64
N_HEADS = D_MODEL // HEAD_DIM
N_KV_HEADS = 8
GQA_GROUP = N_HEADS // N_KV_HEADS
WINDOW = 128
ROPE_THETA = 10000.0
D_FF = 11008
EPS = 1e-6

F32 = jnp.float32
BF16 = jnp.bfloat16

LANES = 128
SUBLANES = 8
VMEM_LIMIT = 56 * 1024 * 1024
Q_COLS = N_HEADS * HEAD_DIM
KV_COLS = N_KV_HEADS * HEAD_DIM
KV_DUP = N_KV_HEADS * LANES
TN_WIDE = 512
TN_PAIR = 256


class _Group(NamedTuple):
    rows: int
    tm: int
    tps: int
    shift: int
    pos0: int
    chunk: int
    hdr_conv: int
    hdr_pool: int
    tm_down: int
    tr: int


def _prompt_group(batch):
    tm = 1024
    return _Group(rows=batch * SEQ, tm=tm, tps=SEQ // tm, shift=1, pos0=0, chunk=256,
                  hdr_conv=SUBLANES, hdr_pool=POOL_CTX + 1, tm_down=512, tr=512)


def _sample_group():
    rows = DEC_BATCH * DEC_SEQ
    return _Group(rows=rows, tm=rows, tps=1, shift=DEC_BATCH, pos0=PAST_LEN, chunk=rows,
                  hdr_conv=2 * DEC_BATCH, hdr_pool=(POOL_CTX + 1) * DEC_BATCH, tm_down=rows, tr=rows)


def _params(n_axes=2):
    return pltpu.CompilerParams(
        dimension_semantics=("arbitrary",) * n_axes, vmem_limit_bytes=VMEM_LIMIT)


def _dot(a, b):
    return jnp.dot(a, b, preferred_element_type=F32)


def _div_pow2(x, d):
    assert d & (d - 1) == 0
    return lax.shift_right_logical(x, d.bit_length() - 1)


def _round_weights(pairs):
    for src, dst in pairs:
        dst[...] = src[...].astype(BF16)


def _norm_kernel(x_ref, g_ref, o_ref):
    x = x_ref[...]
    y = x * lax.rsqrt(jnp.mean(x * x, axis=-1, keepdims=True) + EPS)
    o_ref[...] = (y * g_ref[...]).astype(o_ref.dtype)


def _rmsnorm(x, g, out_dtype, tr):
    rows, d = x.shape
    return pl.pallas_call(
        _norm_kernel,
        grid=(rows // tr,),
        in_specs=[pl.BlockSpec((tr, d), lambda i: (i, 0)),
                  pl.BlockSpec((1, d), lambda i: (0, 0))],
        out_specs=pl.BlockSpec((tr, d), lambda i: (i, 0)),
        out_shape=jax.ShapeDtypeStruct((rows, d), out_dtype),
        compiler_params=_params(1),
        name="rmsnorm",
    )(x, g.reshape(1, d))


def _set_header(z_ref, ctx_ref, first, tm, hdr):
    if first is None:
        z_ref[0:hdr, :] = ctx_ref[...]
        return

    @pl.when(first)
    def _():
        z_ref[0:hdr, :] = ctx_ref[...]

    @pl.when(jnp.logical_not(first))
    def _():
        z_ref[0:hdr, :] = z_ref[tm:tm + hdr, :]


def _conv3(z_ref, cw, cur, lo, shift, rows):
    y = z_ref[lo - 2 * shift:lo - 2 * shift + rows, :] * cw[0:1, :]
    y = y + z_ref[lo - shift:lo - shift + rows, :] * cw[1:2, :]
    return y + cur * cw[2:3, :]


def _group_specs(g, hdr, tn, col_off=0):
    ctx = pl.BlockSpec((hdr, tn), lambda n, m: (m // g.tps, n + col_off))
    outs = [pl.BlockSpec((g.tm, tn), lambda n, m: (m, n)),
            pl.BlockSpec((hdr, tn), lambda n, m: (m, n))]
    return ctx, outs


def _single_specs(g, hdr, tn, col_off=0):
    ctx = pl.BlockSpec((hdr, tn), lambda n, m: (0, n + col_off))
    outs = [pl.BlockSpec((g.tm, tn), lambda n, m: (0, n)),
            pl.BlockSpec((hdr, tn), lambda n, m: (0, n))]
    return ctx, outs


def _lhs_specs(gp, gs, k):
    return [pl.BlockSpec((gp.tm, k), lambda n, m: (m, 0)),
            pl.BlockSpec((gs.tm, k), lambda n, m: (0, 0))]


def _mix_a_rows(h_ref, ctx_ref, y_ref, tail_ref, z_ref, wb, wc, wx, cw, *, g, first):
    tm, hdr = g.tm, g.hdr_conv
    _set_header(z_ref, ctx_ref, first, tm, hdr)
    for r0 in range(0, tm, g.chunk):
        hc = h_ref[r0:r0 + g.chunk, :]
        gate_b = _dot(hc, wb[...])
        u = _dot(hc, wc[...]) * _dot(hc, wx[...])
        z_ref[hdr + r0:hdr + r0 + g.chunk, :] = u
        conv = _conv3(z_ref, cw, u, hdr + r0, g.shift, g.chunk)
        y_ref[r0:r0 + g.chunk, :] = (gate_b * conv).astype(y_ref.dtype)
    tail_ref[...] = z_ref[tm:tm + hdr, :]


def _mix_a_kernel(hp_ref, hs_ref, wb_ref, wc_ref, wx_ref, cw_ref, ctxp_ref, ctxs_ref,
                  yp_ref, tailp_ref, ys_ref, tails_ref,
                  wbb_ref, wcb_ref, wxb_ref, zp_ref, zs_ref, *, gp, gs):
    m = pl.program_id(1)
    cw = cw_ref[...]
    rows = functools.partial(_mix_a_rows, wb=wbb_ref, wc=wcb_ref, wx=wxb_ref, cw=cw)

    @pl.when(m == 0)
    def _():
        _round_weights([(wb_ref, wbb_ref), (wc_ref, wcb_ref), (wx_ref, wxb_ref)])
        rows(hs_ref, ctxs_ref, ys_ref, tails_ref, zs_ref, g=gs, first=None)

    rows(hp_ref, ctxp_ref, yp_ref, tailp_ref, zp_ref, g=gp, first=m % gp.tps == 0)


def _mix_a(hp, hs, w_in, conv_w, ctxp, ctxs, gp, gs):
    tn = TN_PAIR
    nn, nm = D_A // tn, gp.rows // gp.tm
    ctxp_spec, outp = _group_specs(gp, gp.hdr_conv, tn)
    ctxs_spec, outs = _single_specs(gs, gs.hdr_conv, tn)
    wspec = lambda off: pl.BlockSpec((D_MODEL, tn), lambda n, m: (0, n + off))
    return pl.pallas_call(
        functools.partial(_mix_a_kernel, gp=gp, gs=gs),
        grid=(nn, nm),
        in_specs=_lhs_specs(gp, gs, D_MODEL) + [
            wspec(0), wspec(nn), wspec(2 * nn),
            pl.BlockSpec((3, tn), lambda n, m: (0, n)),
            ctxp_spec, ctxs_spec],
        out_specs=outp + outs,
        out_shape=[
            jax.ShapeDtypeStruct((gp.rows, D_A), BF16),
            jax.ShapeDtypeStruct((nm * gp.hdr_conv, D_A), F32),
            jax.ShapeDtypeStruct((gs.rows, D_A), BF16),
            jax.ShapeDtypeStruct((gs.hdr_conv, D_A), F32),
        ],
        scratch_shapes=[pltpu.VMEM((D_MODEL, tn), BF16)] * 3 + [
            pltpu.VMEM((gp.hdr_conv + gp.tm, tn), F32), pltpu.VMEM((gs.hdr_conv + gs.tm, tn), F32)],
        compiler_params=_params(),
        name="mix_a",
    )(hp, hs, w_in, w_in, w_in, conv_w, ctxp, ctxs)


def _mix_b_rows(h_ref, ctx_ref, y_ref, tail_ref, z_ref, wp, wg, scale, *, g, first, tile, win):
    tm, hdr = g.tm, g.hdr_pool
    _set_header(z_ref, ctx_ref, first, tm, hdr)
    for r0 in range(0, tm, g.chunk):
        z_ref[hdr + r0:hdr + r0 + g.chunk, :] = _dot(h_ref[r0:r0 + g.chunk, :], wp[...])
    for r0 in range(0, tm, g.chunk):
        lo = hdr + r0
        p = z_ref[lo:lo + g.chunk, :]
        acc = p
        for i in range(1, win):
            acc = acc + z_ref[lo - i * g.shift:lo - i * g.shift + g.chunk, :]
        row = lax.broadcasted_iota(jnp.int32, (g.chunk, 1), 0) + (tile * tm + r0)
        pos = g.pos0 + _div_pow2(row, g.shift)
        cnt = jnp.minimum(pos + 1, win).astype(F32)
        pooled = acc / cnt - p
        yb = _dot(pooled.astype(BF16), wg[...]) * scale
        y_ref[r0:r0 + g.chunk, :] = yb.astype(y_ref.dtype)
    tail_ref[...] = z_ref[tm:tm + hdr, :]


def _mix_b_kernel(hp_ref, hs_ref, wp_ref, wg_ref, sc_ref, ctxp_ref, ctxs_ref,
                  yp_ref, tailp_ref, ys_ref, tails_ref,
                  wpb_ref, wgb_ref, zp_ref, zs_ref, *, gp, gs):
    grp = pl.program_id(0)
    m = pl.program_id(1)
    scale = sc_ref[...]

    @pl.when(m == 0)
    def _():
        _round_weights([(wp_ref, wpb_ref), (wg_ref, wgb_ref)])

    for gi, win in enumerate(POOL_WINDOWS):
        rows = functools.partial(_mix_b_rows, wp=wpb_ref, wg=wgb_ref, scale=scale, win=win)

        @pl.when(jnp.logical_and(grp == gi, m == 0))
        def _(rows=rows):
            rows(hs_ref, ctxs_ref, ys_ref, tails_ref, zs_ref, g=gs, first=None, tile=0)

        @pl.when(grp == gi)
        def _(rows=rows):
            rows(hp_ref, ctxp_ref, yp_ref, tailp_ref, zp_ref, g=gp,
                 first=m % gp.tps == 0, tile=m % gp.tps)


def _mix_b(hp, hs, w_in, w_grp, scale, ctxp, ctxs, gp, gs):
    tn, ng = POOL_GROUP, len(POOL_WINDOWS)
    nm = gp.rows // gp.tm
    col0 = 3 * D_A // tn
    ctxp_spec, outp = _group_specs(gp, gp.hdr_pool, tn)
    ctxs_spec, outs = _single_specs(gs, gs.hdr_pool, tn)
    return pl.pallas_call(
        functools.partial(_mix_b_kernel, gp=gp, gs=gs),
        grid=(ng, nm),
        in_specs=_lhs_specs(gp, gs, D_MODEL) + [
            pl.BlockSpec((D_MODEL, tn), lambda n, m: (0, col0 + n)),
            pl.BlockSpec((None, tn, tn), lambda n, m: (n, 0, 0)),
            pl.BlockSpec((1, tn), lambda n, m: (0, n)),
            ctxp_spec, ctxs_spec],
        out_specs=outp + outs,
        out_shape=[
            jax.ShapeDtypeStruct((gp.rows, D_B), BF16),
            jax.ShapeDtypeStruct((nm * gp.hdr_pool, D_B), F32),
            jax.ShapeDtypeStruct((gs.rows, D_B), BF16),
            jax.ShapeDtypeStruct((gs.hdr_pool, D_B), F32),
        ],
        scratch_shapes=[
            pltpu.VMEM((D_MODEL, tn), BF16), pltpu.VMEM((tn, tn), BF16),
            pltpu.VMEM((gp.hdr_pool + gp.tm, tn), F32), pltpu.VMEM((gs.hdr_pool + gs.tm, tn), F32)],
        compiler_params=_params(),
        name="mix_b",
    )(hp, hs, w_in, w_grp, scale, ctxp, ctxs)


def _res_mm_kernel(*refs, n_lhs):
    m = pl.program_id(1)
    xp_ref, xs_ref = refs[0], refs[1]
    ap = refs[2:2 + n_lhs]
    a_s = refs[2 + n_lhs:2 + 2 * n_lhs]
    w = refs[2 + 2 * n_lhs:2 + 3 * n_lhs]
    op_ref, os_ref = refs[2 + 3 * n_lhs], refs[3 + 3 * n_lhs]
    wb = refs[4 + 3 * n_lhs:]

    def rows(x_ref, a_refs, o_ref):
        acc = x_ref[...]
        for a_ref, wb_ref in zip(a_refs, wb):
            acc = acc + _dot(a_ref[...], wb_ref[...])
        o_ref[...] = acc

    @pl.when(m == 0)
    def _():
        _round_weights(list(zip(w, wb)))
        rows(xs_ref, a_s, os_ref)

    rows(xp_ref, ap, op_ref)


def _res_mm(xp, xs, lhs_p, lhs_s, w, gp, gs, name):
    d_out = xp.shape[1]
    n_lhs = len(lhs_p)
    kdim = lhs_p[0].shape[1]
    tn = TN_WIDE
    xspec = [pl.BlockSpec((gp.tm, tn), lambda n, m: (m, n)),
             pl.BlockSpec((gs.tm, tn), lambda n, m: (0, n))]
    lhs_specs = _lhs_specs(gp, gs, kdim)
    in_specs = xspec + [lhs_specs[0]] * n_lhs + [lhs_specs[1]] * n_lhs
    in_specs += [pl.BlockSpec((kdim, tn), lambda n, m, i=i: (i, n)) for i in range(n_lhs)]
    return pl.pallas_call(
        functools.partial(_res_mm_kernel, n_lhs=n_lhs),
        grid=(d_out // tn, gp.rows // gp.tm),
        in_specs=in_specs,
        out_specs=xspec,
        out_shape=[jax.ShapeDtypeStruct(xp.shape, F32), jax.ShapeDtypeStruct(xs.shape, F32)],
        scratch_shapes=[pltpu.VMEM((kdim, tn), BF16)] * n_lhs,
        compiler_params=_params(),
        name=name,
    )(xp, xs, *lhs_p, *lhs_s, *([w] * n_lhs))


def _down_kernel(x_ref, a_ref, w_ref, o_ref):
    o_ref[...] = x_ref[...] + _dot(a_ref[...], w_ref[...])


def _ffn_down(x, act, w, layer, *, tm):
    rows, d_out = x.shape
    kdim = act.shape[1]
    tn = TN_WIDE
    return pl.pallas_call(
        _down_kernel,
        grid=(rows // tm, d_out // tn),
        in_specs=[pl.BlockSpec((tm, tn), lambda m, n: (m, n)),
                  pl.BlockSpec((tm, kdim), lambda m, n: (m, 0)),
                  pl.BlockSpec((None, kdim, tn), lambda m, n: (layer, 0, n))],
        out_specs=pl.BlockSpec((tm, tn), lambda m, n: (m, n)),
        out_shape=jax.ShapeDtypeStruct((rows, d_out), F32),
        compiler_params=_params(),
        name="ffn_down",
    )(x, act, w)


def _rope_table_kernel(inv_ref, cos_ref, sin_ref, *, pos0, shift):
    rows = cos_ref.shape[0]
    row = lax.broadcasted_iota(jnp.int32, (rows, LANES), 0)
    lane = lax.broadcasted_iota(jnp.int32, (rows, LANES), 1)
    pos = (pos0 + _div_pow2(row, shift)).astype(F32)
    ang = pos * inv_ref[...]
    first_half = (lane & (HEAD_DIM - 1)) < (HEAD_DIM // 2)
    cos_ref[...] = jnp.cos(ang)
    sin = jnp.sin(ang)
    sin_ref[...] = jnp.where(first_half, -sin, sin)


def _rope_table(rows, pos0, shift):
    half = HEAD_DIM // 2
    inv = ROPE_THETA ** (-jnp.arange(half, dtype=F32) / half)
    inv = jnp.tile(inv, LANES // half).reshape(1, LANES)
    return pl.pallas_call(
        functools.partial(_rope_table_kernel, pos0=pos0, shift=shift),
        out_shape=[jax.ShapeDtypeStruct((rows, LANES), F32)] * 2,
        name="rope_table",
    )(inv)


def _rope(x, cos_ref, sin_ref, r0):
    rows, tn = x.shape
    reps = tn // LANES
    cos = jnp.concatenate([cos_ref[r0:r0 + rows, :]] * reps, axis=1)
    sin = jnp.concatenate([sin_ref[r0:r0 + rows, :]] * reps, axis=1)
    lane = lax.broadcasted_iota(jnp.int32, x.shape, 1)
    half = HEAD_DIM // 2
    first_half = (lane & (HEAD_DIM - 1)) < half
    partner = jnp.where(first_half, pltpu.roll(x, tn - half, 1), pltpu.roll(x, half, 1))
    return x * cos + partner * sin


def _dup_heads(x):
    lane = lax.broadcasted_iota(jnp.int32, (x.shape[0], LANES), 1)
    lower = lane < HEAD_DIM
    out = []
    for c in range(x.shape[1] // LANES):
        v = x[:, c * LANES:(c + 1) * LANES]
        r = pltpu.roll(v, HEAD_DIM, 1)
        out += [jnp.where(lower, v, r), jnp.where(lower, r, v)]
    return jnp.concatenate(out, axis=1)


def _q_rows(h_ref, cos_ref, sin_ref, q_ref, wq, bias, *, g):
    for r0 in range(0, g.tm, g.chunk):
        acc = _dot(h_ref[r0:r0 + g.chunk, :], wq[...]) + bias
        roped = _rope(acc, cos_ref, sin_ref, r0)
        q_ref[r0:r0 + g.chunk, :] = (roped * (HEAD_DIM ** -0.5)).astype(q_ref.dtype)


def _q_kernel(hp_ref, hs_ref, w_ref, b_ref, cosp_ref, sinp_ref, coss_ref, sins_ref,
              qp_ref, qs_ref, wb_ref, *, gp, gs):
    m = pl.program_id(1)
    bias = b_ref[...]

    @pl.when(m == 0)
    def _():
        _round_weights([(w_ref, wb_ref)])
        _q_rows(hs_ref, coss_ref, sins_ref, qs_ref, wb_ref, bias, g=gs)

    _q_rows(hp_ref, cosp_ref, sinp_ref, qp_ref, wb_ref, bias, g=gp)


def _kv_rows(h_ref, cos_ref, sin_ref, kv_ref, wkv, bias, is_key, *, g):
    for r0 in range(0, g.tm, g.chunk):
        acc = _dot(h_ref[r0:r0 + g.chunk, :], wkv[...]) + bias
        roped = _rope(acc, cos_ref, sin_ref, r0)
        kv_ref[r0:r0 + g.chunk, :] = _dup_heads(jnp.where(is_key, roped, acc))


def _kv_kernel(hp_ref, hs_ref, w_ref, b_ref, cosp_ref, sinp_ref, coss_ref, sins_ref,
               kvp_ref, kvs_ref, wb_ref, *, gp, gs):
    is_key = pl.program_id(0) == 0
    m = pl.program_id(1)
    bias = b_ref[...]

    @pl.when(m == 0)
    def _():
        _round_weights([(w_ref, wb_ref)])
        _kv_rows(hs_ref, coss_ref, sins_ref, kvs_ref, wb_ref, bias, is_key, g=gs)

    _kv_rows(hp_ref, cosp_ref, sinp_ref, kvp_ref, wb_ref, bias, is_key, g=gp)


def _qkv(hp, hs, w_qkv, b_qkv, tabp, tabs, gp, gs):
    tn = TN_WIDE
    nm = gp.rows // gp.tm
    tab_specs = [pl.BlockSpec((gp.tm, LANES), lambda n, m: (m % gp.tps, 0))] * 2
    tab_specs += [pl.BlockSpec((gs.tm, LANES), lambda n, m: (0, 0))] * 2

    def call(kern, n_tiles, col_off, out_tn, out_cols, dtype, name):
        return pl.pallas_call(
            functools.partial(kern, gp=gp, gs=gs),
            grid=(n_tiles, nm),
            in_specs=_lhs_specs(gp, gs, D_MODEL) + [
                pl.BlockSpec((D_MODEL, tn), lambda n, m: (0, n + col_off)),
                pl.BlockSpec((1, tn), lambda n, m: (0, n + col_off))] + tab_specs,
            out_specs=[pl.BlockSpec((gp.tm, out_tn), lambda n, m: (m, n)),
                       pl.BlockSpec((gs.tm, out_tn), lambda n, m: (0, n))],
            out_shape=[jax.ShapeDtypeStruct((gp.rows, out_cols), dtype),
                       jax.ShapeDtypeStruct((gs.rows, out_cols), dtype)],
            scratch_shapes=[pltpu.VMEM((D_MODEL, tn), BF16)],
            compiler_params=_params(),
            name=name,
        )(hp, hs, w_qkv, b_qkv, *tabp, *tabs)

    assert KV_COLS == tn
    qp, qs = call(_q_kernel, Q_COLS // tn, 0, tn, Q_COLS, BF16, "q_rope")
    kvp, kvs = call(_kv_kernel, 2, Q_COLS // tn, KV_DUP, 2 * KV_DUP, F32, "kv_rope")
    return qp, qs, kvp, kvs


def _block_diag(x2):
    lane = lax.broadcasted_iota(jnp.int32, x2.shape, 1)
    lo = jnp.where(lane < HEAD_DIM, x2, 0.0)
    hi = jnp.where(lane >= HEAD_DIM, x2, 0.0)
    return jnp.concatenate([lo, hi], axis=0).astype(BF16)


def _attn_kernel(q_ref, kvp_ref, kvc_ref, sink_ref, o_ref, *, blocks_per_seq):
    i = pl.program_id(0)
    pairs = GQA_GROUP // 2
    rows = pairs * WINDOW
    prev_bias = jnp.where((i % blocks_per_seq) > 0, 0.0, -jnp.inf)
    qi = lax.broadcasted_iota(jnp.int32, (rows, 2 * WINDOW), 0) & (WINDOW - 1)
    kj = lax.broadcasted_iota(jnp.int32, (rows, 2 * WINDOW), 1) & (WINDOW - 1)
    cur = kj <= qi
    lower = lax.broadcasted_iota(jnp.int32, (rows, LANES), 1) < HEAD_DIM
    ones_bd = _block_diag(jnp.ones((WINDOW, LANES), F32))
    nt = (((1,), (1,)), ((), ()))
    def qcols(kh):
        return [slice((kh * pairs + pi) * LANES, (kh * pairs + pi + 1) * LANES) for pi in range(pairs)]

    def scores(kh):
        kc = slice(kh * LANES, (kh + 1) * LANES)
        q = jnp.concatenate([q_ref[:, qc] for qc in qcols(kh)], axis=0)
        s_cur = lax.dot_general(q, _block_diag(kvc_ref[:, kc]), nt, preferred_element_type=F32)
        s_prev = lax.dot_general(q, _block_diag(kvp_ref[:, kc]), nt, preferred_element_type=F32)
        return jnp.where(cur, s_cur, s_prev + prev_bias)

    def finish(kh, s):
        vc = slice(KV_DUP + kh * LANES, KV_DUP + (kh + 1) * LANES)
        sink = sink_ref[kh]
        ps, es = [], []
        for hd in range(2):
            sh = s[:, hd * WINDOW:(hd + 1) * WINDOW]
            sk = sink[:, hd * HEAD_DIM:hd * HEAD_DIM + 1]
            mx = jnp.maximum(jnp.max(sh, axis=-1, keepdims=True), sk)
            ps.append(jnp.exp(sh - mx))
            es.append(jnp.exp(sk - mx))
        p = jnp.concatenate(ps, axis=1).astype(BF16)
        zero = jnp.zeros_like(p)
        o = _dot(jnp.where(cur, p, zero), _block_diag(kvc_ref[:, vc]))
        o = o + _dot(jnp.where(cur, zero, p), _block_diag(kvp_ref[:, vc]))
        den = _dot(p, ones_bd) + jnp.where(lower, es[0], es[1])
        o = (o / den).astype(o_ref.dtype)
        for pi, qc in enumerate(qcols(kh)):
            o_ref[:, qc] = o[pi * WINDOW:(pi + 1) * WINDOW, :]

    ahead = 8
    pending = [scores(kh) for kh in range(ahead)]
    for kh in range(N_KV_HEADS):
        if kh + ahead < N_KV_HEADS:
            pending.append(scores(kh + ahead))
        finish(kh, pending.pop(0))


def _attn_prompt(q, kv, sink_t):
    rows = q.shape[0]
    bps = SEQ // WINDOW
    return pl.pallas_call(
        functools.partial(_attn_kernel, blocks_per_seq=bps),
        grid=(rows // WINDOW,),
        in_specs=[
            pl.BlockSpec((WINDOW, Q_COLS), lambda i: (i, 0)),
            pl.BlockSpec((WINDOW, 2 * KV_DUP), lambda i: (jnp.where(i % bps == 0, i, i - 1), 0)),
            pl.BlockSpec((WINDOW, 2 * KV_DUP), lambda i: (i, 0)),
            pl.BlockSpec(sink_t.shape, lambda i: (0, 0, 0)),
        ],
        out_specs=pl.BlockSpec((WINDOW, Q_COLS), lambda i: (i, 0)),
        out_shape=jax.ShapeDtypeStruct((rows, Q_COLS), BF16),
        compiler_params=_params(1),
        name="attn_banded",
    )(q, kv, kv, sink_t)


def _attn_sample_kernel(q_ref, k_ref, v_ref, sink_ref, o_ref, *, n_keys, kpos0):
    rows = GQA_GROUP * DEC_SEQ
    keys = k_ref.shape[2]
    t = lax.broadcasted_iota(jnp.int32, (rows, keys), 0) % DEC_SEQ
    j = lax.broadcasted_iota(jnp.int32, (rows, keys), 1)
    mask = (j > t) & (j <= t + WINDOW) & (j < n_keys) & (j + kpos0 >= 0)
    scores = [lax.dot_general(q_ref[0, kh], k_ref[0, kh].astype(BF16), (((1,), (1,)), ((), ())),
                              preferred_element_type=F32) for kh in range(N_KV_HEADS)]
    for kh in range(N_KV_HEADS):
        s = jnp.where(mask, scores[kh], -jnp.inf)
        sk = sink_ref[kh][:, 0:1]
        mx = jnp.maximum(jnp.max(s, axis=-1, keepdims=True), sk)
        p = jnp.exp(s - mx)
        d = jnp.sum(p, axis=-1, keepdims=True) + jnp.exp(sk - mx)
        o = _dot(p.astype(BF16), v_ref[0, kh].astype(BF16))
        o_ref[0, kh] = (o / d).astype(o_ref.dtype)


def _attn_sample(q, k, v, sink_s, n_keys):
    nb, nkv, rows, d = q.shape
    keys = k.shape[2]
    return pl.pallas_call(
        functools.partial(_attn_sample_kernel, n_keys=n_keys, kpos0=PAST_LEN - WINDOW),
        grid=(nb,),
        in_specs=[
            pl.BlockSpec((1, nkv, rows, d), lambda b: (b, 0, 0, 0)),
            pl.BlockSpec((1, nkv, keys, d), lambda b: (b, 0, 0, 0)),
            pl.BlockSpec((1, nkv, keys, d), lambda b: (b, 0, 0, 0)),
            pl.BlockSpec(sink_s.shape, lambda b: (0, 0, 0)),
        ],
        out_specs=pl.BlockSpec((1, nkv, rows, d), lambda b: (b, 0, 0, 0)),
        out_shape=jax.ShapeDtypeStruct(q.shape, BF16),
        compiler_params=_params(1),
        name="attn_sample",
    )(q, k, v, sink_s)


def _ffn_up_rows(h_ref, ctxg_ref, ctxu_ref, act_ref, tailg_ref, tailu_ref, z_ref, w, cw, *, g, first):
    tm, hdr = g.tm, g.hdr_conv
    tn = act_ref.shape[1]
    if first is None:
        z_ref[0:hdr, 0:tn] = ctxg_ref[...]
        z_ref[0:hdr, tn:2 * tn] = ctxu_ref[...]
    else:
        @pl.when(first)
        def _():
            z_ref[0:hdr, 0:tn] = ctxg_ref[...]
            z_ref[0:hdr, tn:2 * tn] = ctxu_ref[...]

        @pl.when(jnp.logical_not(first))
        def _():
            z_ref[0:hdr, :] = z_ref[tm:tm + hdr, :]

    for r0 in range(0, tm, g.chunk):
        up = _dot(h_ref[r0:r0 + g.chunk, :], w[...])
        lo = hdr + r0
        z_ref[lo:lo + g.chunk, :] = up
        y = _conv3(z_ref, cw, up, lo, g.shift, g.chunk)
        gate, lin = y[:, 0:tn], y[:, tn:2 * tn]
        silu = gate / (1.0 + jnp.exp(-gate))
        act_ref[r0:r0 + g.chunk, :] = (silu * lin).astype(act_ref.dtype)
    tailg_ref[...] = z_ref[tm:tm + hdr, 0:tn]
    tailu_ref[...] = z_ref[tm:tm + hdr, tn:2 * tn]


def _ffn_up_kernel(hp_ref, hs_ref, wg_ref, wu_ref, cwg_ref, cwu_ref,
                   ctxgp_ref, ctxup_ref, ctxgs_ref, ctxus_ref,
                   actp_ref, tailgp_ref, tailup_ref, acts_ref, tailgs_ref, tailus_ref,
                   wb_ref, zp_ref, zs_ref, *, gp, gs):
    m = pl.program_id(1)
    tn = wg_ref.shape[1]
    cw = jnp.concatenate([cwg_ref[...], cwu_ref[...]], axis=1)
    rows = functools.partial(_ffn_up_rows, w=wb_ref, cw=cw)

    @pl.when(m == 0)
    def _():
        wb_ref[:, 0:tn] = wg_ref[...].astype(BF16)
        wb_ref[:, tn:2 * tn] = wu_ref[...].astype(BF16)
        rows(hs_ref, ctxgs_ref, ctxus_ref, acts_ref, tailgs_ref, tailus_ref, zs_ref, g=gs, first=None)

    rows(hp_ref, ctxgp_ref, ctxup_ref, actp_ref, tailgp_ref, tailup_ref, zp_ref,
         g=gp, first=m % gp.tps == 0)


def _ffn_up(hp, hs, w_up, conv_w, ctxp, ctxs, layer, gp, gs):
    tn = TN_PAIR
    nn, nm = D_FF // tn, gp.rows // gp.tm
    ctxgp, outp = _group_specs(gp, gp.hdr_conv, tn)
    ctxup, _ = _group_specs(gp, gp.hdr_conv, tn, nn)
    ctxgs, outs = _single_specs(gs, gs.hdr_conv, tn)
    ctxus, _ = _single_specs(gs, gs.hdr_conv, tn, nn)
    lo = lambda n, m: (layer, 0, n)
    hi = lambda n, m: (layer, 0, n + nn)
    return pl.pallas_call(
        functools.partial(_ffn_up_kernel, gp=gp, gs=gs),
        grid=(nn, nm),
        in_specs=_lhs_specs(gp, gs, D_MODEL) + [
            pl.BlockSpec((None, D_MODEL, tn), lo), pl.BlockSpec((None, D_MODEL, tn), hi),
            pl.BlockSpec((None, 3, tn), lo), pl.BlockSpec((None, 3, tn), hi),
            ctxgp, ctxup, ctxgs, ctxus],
        out_specs=outp + outp[1:] + outs + outs[1:],
        out_shape=[
            jax.ShapeDtypeStruct((gp.rows, D_FF), BF16),
            jax.ShapeDtypeStruct((nm * gp.hdr_conv, D_FF), F32),
            jax.ShapeDtypeStruct((nm * gp.hdr_conv, D_FF), F32),
            jax.ShapeDtypeStruct((gs.rows, D_FF), BF16),
            jax.ShapeDtypeStruct((gs.hdr_conv, D_FF), F32),
            jax.ShapeDtypeStruct((gs.hdr_conv, D_FF), F32),
        ],
        scratch_shapes=[pltpu.VMEM((D_MODEL, 2 * tn), BF16),
                        pltpu.VMEM((gp.hdr_conv + gp.tm, 2 * tn), F32),
                        pltpu.VMEM((gs.hdr_conv + gs.tm, 2 * tn), F32)],
        compiler_params=_params(),
        name="ffn_up",
    )(hp, hs, w_up, w_up, conv_w, conv_w, ctxp, ctxp, ctxs, ctxs)


def _norm2(xp, xs, g, gp, gs, dtype=BF16):
    return _rmsnorm(xp, g, dtype, gp.tr), _rmsnorm(xs, g, dtype, gs.tr)


def _ffn(xp, xs, gp, gs, layer, norm_g, w_up, conv_w, w_down_bf16, ctxp, ctxs):
    hp, hs = _norm2(xp, xs, norm_g, gp, gs)
    actp, tgp, tup, acts, tgs, tus = _ffn_up(hp, hs, w_up, conv_w, ctxp, ctxs, layer, gp, gs)
    xp = _ffn_down(xp, actp, w_down_bf16, layer, tm=gp.tm_down)
    xs = _ffn_down(xs, acts, w_down_bf16, layer, tm=gs.tm_down)
    return xp, xs, jnp.concatenate([tgp, tup], axis=1), jnp.concatenate([tgs, tus], axis=1)


def _undup_heads(kv_half):
    rows = kv_half.shape[0]
    return kv_half.reshape(rows, N_KV_HEADS, 2, HEAD_DIM)[:, :, 0, :].reshape(rows, KV_COLS)


def kernel(x_prompt, x_sample, state_conv_a, state_pool, cache_win_k, cache_win_v, state_ffn_conv,
           norm_mix, w_in_ab, conv_a, w_pool, pool_scale, w_out_ab,
           w_qkv, b_qkv, sinks, w_o, norm_ffn, w_up, conv_ffn, w_down, norm_final):
    bp = x_prompt.shape[0]
    nb = x_sample.shape[0]
    gp, gs = _prompt_group(bp), _sample_group()
    tiles = gp.rows // gp.tm
    last_tiles = slice(gp.tps - 1, tiles, gp.tps)

    def to_tm(a):
        return jnp.swapaxes(a, 0, 1).reshape(-1, a.shape[-1])

    def from_tm(a):
        return jnp.swapaxes(a.reshape(-1, nb, a.shape[-1]), 0, 1)

    def pad_hdr(a, hdr):
        return jnp.concatenate([jnp.zeros((hdr - a.shape[0], a.shape[1]), F32), a], axis=0)

    def last_rows(tail, hdr, keep):
        return tail.reshape(tiles, hdr, -1)[last_tiles, hdr - keep:]

    def last_steps(tail, keep):
        return from_tm(tail[tail.shape[0] - keep * nb:])

    sink = sinks[0].astype(F32)
    sink_t = jnp.broadcast_to(
        sink.reshape(N_KV_HEADS, GQA_GROUP // 2, 1, 2, 1),
        (N_KV_HEADS, GQA_GROUP // 2, WINDOW, 2, HEAD_DIM)).reshape(N_KV_HEADS, -1, LANES)
    sink_s = jnp.broadcast_to(
        jnp.repeat(sink.reshape(N_KV_HEADS, GQA_GROUP), DEC_SEQ, axis=1)[:, :, None],
        (N_KV_HEADS, GQA_GROUP * DEC_SEQ, LANES))

    xp = x_prompt.reshape(gp.rows, D_MODEL)
    xs = to_tm(x_sample)
    zeros_conv = lambda c: jnp.zeros((bp * gp.hdr_conv, c), F32)

    hp, hs = _norm2(xp, xs, norm_mix[0], gp, gs)
    yap, tail_ap, yas, tail_as = _mix_a(
        hp, hs, w_in_ab[0], conv_a[0], zeros_conv(D_A), to_tm(state_conv_a[0]), gp, gs)
    ybp, tail_pp, ybs, tail_ps = _mix_b(
        hp, hs, w_in_ab[0], w_pool[0], pool_scale[0].reshape(1, D_B),
        jnp.zeros((bp * gp.hdr_pool, D_B), F32), pad_hdr(to_tm(state_pool[0]), gs.hdr_pool), gp, gs)
    xp, xs = _res_mm(xp, xs, [yap, ybp], [yas, ybs], w_out_ab[0], gp, gs, "out_proj")
    w_down_bf16 = w_down.astype(BF16)
    xp, xs, tail_f0p, tail_f0s = _ffn(xp, xs, gp, gs, 0, norm_ffn[0], w_up, conv_ffn, w_down_bf16,
                                      zeros_conv(2 * D_FF), to_tm(state_ffn_conv[0]))

    hp, hs = _norm2(xp, xs, norm_mix[1], gp, gs)
    tabp = _rope_table(gp.tm * gp.tps, gp.pos0, gp.shift)
    tabs = _rope_table(gs.tm, gs.pos0, gs.shift)
    qp, qs, kvp, kvs = _qkv(hp, hs, w_qkv[0], b_qkv[0].reshape(1, -1), tabp, tabs, gp, gs)

    op = _attn_prompt(qp, kvp, sink_t)
    kv_last = kvp.reshape(bp, SEQ, 2 * KV_DUP)[:, SEQ - WINDOW:].reshape(bp * WINDOW, 2 * KV_DUP)
    k_p = _undup_heads(kv_last[:, :KV_DUP]).reshape(bp, WINDOW, N_KV_HEADS, HEAD_DIM)
    v_p = _undup_heads(kv_last[:, KV_DUP:]).reshape(bp, WINDOW, N_KV_HEADS, HEAD_DIM)

    n_keys = WINDOW + DEC_SEQ
    kk = jnp.concatenate([cache_win_k[0].reshape(nb, WINDOW, KV_COLS),
                          from_tm(_undup_heads(kvs[:, :KV_DUP]))], axis=1)
    vv = jnp.concatenate([cache_win_v[0].reshape(nb, WINDOW, KV_COLS),
                          from_tm(_undup_heads(kvs[:, KV_DUP:]))], axis=1)

    def heads_major(a):
        a = jnp.pad(a, ((0, 0), (0, 2 * WINDOW - n_keys), (0, 0)))
        return a.reshape(nb, 2 * WINDOW, N_KV_HEADS, HEAD_DIM).transpose(0, 2, 1, 3)

    q5 = from_tm(qs).reshape(nb, DEC_SEQ, N_KV_HEADS, GQA_GROUP, HEAD_DIM)
    q5 = q5.transpose(0, 2, 3, 1, 4).reshape(nb, N_KV_HEADS, GQA_GROUP * DEC_SEQ, HEAD_DIM)
    o5 = _attn_sample(q5, heads_major(kk), heads_major(vv), sink_s, n_keys)
    o5 = o5.reshape(nb, N_KV_HEADS, GQA_GROUP, DEC_SEQ, HEAD_DIM).transpose(0, 3, 1, 2, 4)
    os_ = to_tm(o5.reshape(nb, DEC_SEQ, Q_COLS))
    k_s = kk[:, n_keys - WINDOW:].reshape(nb, WINDOW, N_KV_HEADS, HEAD_DIM)
    v_s = vv[:, n_keys - WINDOW:].reshape(nb, WINDOW, N_KV_HEADS, HEAD_DIM)

    xp, xs = _res_mm(xp, xs, [op], [os_], w_o[0], gp, gs, "attn_out")
    xp, xs, tail_f1p, tail_f1s = _ffn(xp, xs, gp, gs, 1, norm_ffn[1], w_up, conv_ffn, w_down_bf16,
                                      zeros_conv(2 * D_FF), to_tm(state_ffn_conv[1]))
    yp, ys = _norm2(xp, xs, norm_final, gp, gs, F32)

    ca_p = last_rows(tail_ap, gp.hdr_conv, 2)[None]
    pl_p = last_rows(tail_pp, gp.hdr_pool, POOL_CTX)[None]
    f_p = jnp.stack([last_rows(t, gp.hdr_conv, 2) for t in (tail_f0p, tail_f1p)])
    ca_s = last_steps(tail_as, 2)[None]
    pl_s = last_steps(tail_ps, POOL_CTX)[None]
    f_s = jnp.stack([last_steps(t, 2) for t in (tail_f0s, tail_f1s)])
    return (yp.reshape(bp, SEQ, D_MODEL), from_tm(ys), ca_p, ca_s, pl_p, pl_s,
            k_p[None], k_s[None], v_p[None], v_s[None], f_p, f_s)
```

```python
import functools
from typing import NamedTuple

import jax
import jax.numpy as jnp
from jax import lax
from jax.experimental import pallas as pl
from jax.experimental.pallas import tpu as pltpu

D_MODEL = 4096
SEQ = 2048
DEC_BATCH = 32
DEC_SEQ = 4
PAST_LEN = 16384
D_A = D_MODEL // 2
D_B = D_MODEL // 2
POOL_WINDOWS = (2, 4, 8, 16)
POOL_GROUP = D_B // len(POOL_WINDOWS)
POOL_CTX = max(POOL_WINDOWS) - 1
HEAD_DIM = 64
N_HEADS = D_MODEL // HEAD_DIM
N_KV_HEADS = 8
GQA_GROUP = N_HEADS // N_KV_HEADS
WINDOW = 128
ROPE_THETA = 10000.0
D_FF = 11008
EPS = 1e-6

F32 = jnp.float32
BF16 = jnp.bfloat16

LANES = 128
SUBLANES = 8
VMEM_LIMIT = 56 * 1024 * 1024
Q_COLS = N_HEADS * HEAD_DIM
KV_COLS = N_KV_HEADS * HEAD_DIM
KV_DUP = N_KV_HEADS * LANES
TN_WIDE = 512
TN_PAIR = 256


class _Group(NamedTuple):
    rows: int
    tm: int
    tps: int
    shift: int
    pos0: int
    chunk: int
    hdr_conv: int
    hdr_pool: int
    tm_down: int
    tr: int


def _prompt_group(batch):
    tm = 1024
    return _Group(rows=batch * SEQ, tm=tm, tps=SEQ // tm, shift=1, pos0=0, chunk=256,
                  hdr_conv=SUBLANES, hdr_pool=POOL_CTX + 1, tm_down=512, tr=512)


def _sample_group():
    rows = DEC_BATCH * DEC_SEQ
    return _Group(rows=rows, tm=rows, tps=1, shift=DEC_BATCH, pos0=PAST_LEN, chunk=rows,
                  hdr_conv=2 * DEC_BATCH, hdr_pool=(POOL_CTX + 1) * DEC_BATCH, tm_down=rows, tr=rows)


def _params(n_axes=2):
    return pltpu.CompilerParams(
        dimension_semantics=("arbitrary",) * n_axes, vmem_limit_bytes=VMEM_LIMIT)


def _dot(a, b):
    return jnp.dot(a, b, preferred_element_type=F32)


def _div_pow2(x, d):
    assert d & (d - 1) == 0
    return lax.shift_right_logical(x, d.bit_length() - 1)


def _round_weights(pairs):
    for src, dst in pairs:
        dst[...] = src[...].astype(BF16)


def _norm_kernel(x_ref, g_ref, o_ref):
    x = x_ref[...]
    y = x * lax.rsqrt(jnp.mean(x * x, axis=-1, keepdims=True) + EPS)
    o_ref[...] = (y * g_ref[...]).astype(o_ref.dtype)


def _rmsnorm(x, g, out_dtype, tr):
    rows, d = x.shape
    return pl.pallas_call(
        _norm_kernel,
        grid=(rows // tr,),
        in_specs=[pl.BlockSpec((tr, d), lambda i: (i, 0)),
                  pl.BlockSpec((1, d), lambda i: (0, 0))],
        out_specs=pl.BlockSpec((tr, d), lambda i: (i, 0)),
        out_shape=jax.ShapeDtypeStruct((rows, d), out_dtype),
        compiler_params=_params(1),
        name="rmsnorm",
    )(x, g.reshape(1, d))


def _set_header(z_ref, ctx_ref, first, tm, hdr):
    if first is None:
        z_ref[0:hdr, :] = ctx_ref[...]
        return

    @pl.when(first)
    def _():
        z_ref[0:hdr, :] = ctx_ref[...]

    @pl.when(jnp.logical_not(first))
    def _():
        z_ref[0:hdr, :] = z_ref[tm:tm + hdr, :]


def _conv3(z_ref, cw, cur, lo, shift, rows):
    y = z_ref[lo - 2 * shift:lo - 2 * shift + rows, :] * cw[0:1, :]
    y = y + z_ref[lo - shift:lo - shift + rows, :] * cw[1:2, :]
    return y + cur * cw[2:3, :]


def _group_specs(g, hdr, tn, col_off=0):
    ctx = pl.BlockSpec((hdr, tn), lambda n, m: (m // g.tps, n + col_off))
    outs = [pl.BlockSpec((g.tm, tn), lambda n, m: (m, n)),
            pl.BlockSpec((hdr, tn), lambda n, m: (m, n))]
    return ctx, outs


def _single_specs(g, hdr, tn, col_off=0):
    ctx = pl.BlockSpec((hdr, tn), lambda n, m: (0, n + col_off))
    outs = [pl.BlockSpec((g.tm, tn), lambda n, m: (0, n)),
            pl.BlockSpec((hdr, tn), lambda n, m: (0, n))]
    return ctx, outs


def _lhs_specs(gp, gs, k):
    return [pl.BlockSpec((gp.tm, k), lambda n, m: (m, 0)),
            pl.BlockSpec((gs.tm, k), lambda n, m: (0, 0))]


def _mix_a_rows(h_ref, ctx_ref, y_ref, tail_ref, z_ref, wb, wc, wx, cw, *, g, first):
    tm, hdr = g.tm, g.hdr_conv
    _set_header(z_ref, ctx_ref, first, tm, hdr)
    for r0 in range(0, tm, g.chunk):
        hc = h_ref[r0:r0 + g.chunk, :]
        gate_b = _dot(hc, wb[...])
        u = _dot(hc, wc[...]) * _dot(hc, wx[...])
        z_ref[hdr + r0:hdr + r0 + g.chunk, :] = u
        conv = _conv3(z_ref, cw, u, hdr + r0, g.shift, g.chunk)
        y_ref[r0:r0 + g.chunk, :] = (gate_b * conv).astype(y_ref.dtype)
    tail_ref[...] = z_ref[tm:tm + hdr, :]


def _mix_a_kernel(hp_ref, hs_ref, wb_ref, wc_ref, wx_ref, cw_ref, ctxp_ref, ctxs_ref,
                  yp_ref, tailp_ref, ys_ref, tails_ref,
                  wbb_ref, wcb_ref, wxb_ref, zp_ref, zs_ref, *, gp, gs):
    m = pl.program_id(1)
    cw = cw_ref[...]
    rows = functools.partial(_mix_a_rows, wb=wbb_ref, wc=wcb_ref, wx=wxb_ref, cw=cw)

    @pl.when(m == 0)
    def _():
        _round_weights([(wb_ref, wbb_ref), (wc_ref, wcb_ref), (wx_ref, wxb_ref)])
        rows(hs_ref, ctxs_ref, ys_ref, tails_ref, zs_ref, g=gs, first=None)

    rows(hp_ref, ctxp_ref, yp_ref, tailp_ref, zp_ref, g=gp, first=m % gp.tps == 0)


def _mix_a(hp, hs, w_in, conv_w, ctxp, ctxs, gp, gs):
    tn = TN_PAIR
    nn, nm = D_A // tn, gp.rows // gp.tm
    ctxp_spec, outp = _group_specs(gp, gp.hdr_conv, tn)
    ctxs_spec, outs = _single_specs(gs, gs.hdr_conv, tn)
    wspec = lambda off: pl.BlockSpec((D_MODEL, tn), lambda n, m: (0, n + off))
    return pl.pallas_call(
        functools.partial(_mix_a_kernel, gp=gp, gs=gs),
        grid=(nn, nm),
        in_specs=_lhs_specs(gp, gs, D_MODEL) + [
            wspec(0), wspec(nn), wspec(2 * nn),
            pl.BlockSpec((3, tn), lambda n, m: (0, n)),
            ctxp_spec, ctxs_spec],
        out_specs=outp + outs,
        out_shape=[
            jax.ShapeDtypeStruct((gp.rows, D_A), BF16),
            jax.ShapeDtypeStruct((nm * gp.hdr_conv, D_A), F32),
            jax.ShapeDtypeStruct((gs.rows, D_A), BF16),
            jax.ShapeDtypeStruct((gs.hdr_conv, D_A), F32),
        ],
        scratch_shapes=[pltpu.VMEM((D_MODEL, tn), BF16)] * 3 + [
            pltpu.VMEM((gp.hdr_conv + gp.tm, tn), F32), pltpu.VMEM((gs.hdr_conv + gs.tm, tn), F32)],
        compiler_params=_params(),
        name="mix_a",
    )(hp, hs, w_in, w_in, w_in, conv_w, ctxp, ctxs)


def _mix_b_rows(h_ref, ctx_ref, y_ref, tail_ref, z_ref, wp, wg, scale, *, g, first, tile, win):
    tm, hdr = g.tm, g.hdr_pool
    _set_header(z_ref, ctx_ref, first, tm, hdr)
    for r0 in range(0, tm, g.chunk):
        z_ref[hdr + r0:hdr + r0 + g.chunk, :] = _dot(h_ref[r0:r0 + g.chunk, :], wp[...])
    for r0 in range(0, tm, g.chunk):
        lo = hdr + r0
        p = z_ref[lo:lo + g.chunk, :]
        acc = p
        for i in range(1, win):
            acc = acc + z_ref[lo - i * g.shift:lo - i * g.shift + g.chunk, :]
        row = lax.broadcasted_iota(jnp.int32, (g.chunk, 1), 0) + (tile * tm + r0)
        pos = g.pos0 + _div_pow2(row, g.shift)
        cnt = jnp.minimum(pos + 1, win).astype(F32)
        pooled = acc / cnt - p
        yb = _dot(pooled.astype(BF16), wg[...]) * scale
        y_ref[r0:r0 + g.chunk, :] = yb.astype(y_ref.dtype)
    tail_ref[...] = z_ref[tm:tm + hdr, :]


def _mix_b_kernel(hp_ref, hs_ref, wp_ref, wg_ref, sc_ref, ctxp_ref, ctxs_ref,
                  yp_ref, tailp_ref, ys_ref, tails_ref,
                  wpb_ref, wgb_ref, zp_ref, zs_ref, *, gp, gs):
    grp = pl.program_id(0)
    m = pl.program_id(1)
    scale = sc_ref[...]

    @pl.when(m == 0)
    def _():
        _round_weights([(wp_ref, wpb_ref), (wg_ref, wgb_ref)])

    for gi, win in enumerate(POOL_WINDOWS):
        rows = functools.partial(_mix_b_rows, wp=wpb_ref, wg=wgb_ref, scale=scale, win=win)

        @pl.when(jnp.logical_and(grp == gi, m == 0))
        def _(rows=rows):
            rows(hs_ref, ctxs_ref, ys_ref, tails_ref, zs_ref, g=gs, first=None, tile=0)

        @pl.when(grp == gi)
        def _(rows=rows):
            rows(hp_ref, ctxp_ref, yp_ref, tailp_ref, zp_ref, g=gp,
                 first=m % gp.tps == 0, tile=m % gp.tps)


def _mix_b(hp, hs, w_in, w_grp, scale, ctxp, ctxs, gp, gs):
    tn, ng = POOL_GROUP, len(POOL_WINDOWS)
    nm = gp.rows // gp.tm
    col0 = 3 * D_A // tn
    ctxp_spec, outp = _group_specs(gp, gp.hdr_pool, tn)
    ctxs_spec, outs = _single_specs(gs, gs.hdr_pool, tn)
    return pl.pallas_call(
        functools.partial(_mix_b_kernel, gp=gp, gs=gs),
        grid=(ng, nm),
        in_specs=_lhs_specs(gp, gs, D_MODEL) + [
            pl.BlockSpec((D_MODEL, tn), lambda n, m: (0, col0 + n)),
            pl.BlockSpec((None, tn, tn), lambda n, m: (n, 0, 0)),
            pl.BlockSpec((1, tn), lambda n, m: (0, n)),
            ctxp_spec, ctxs_spec],
        out_specs=outp + outs,
        out_shape=[
            jax.ShapeDtypeStruct((gp.rows, D_B), BF16),
            jax.ShapeDtypeStruct((nm * gp.hdr_pool, D_B), F32),
            jax.ShapeDtypeStruct((gs.rows, D_B), BF16),
            jax.ShapeDtypeStruct((gs.hdr_pool, D_B), F32),
        ],
        scratch_shapes=[
            pltpu.VMEM((D_MODEL, tn), BF16), pltpu.VMEM((tn, tn), BF16),
            pltpu.VMEM((gp.hdr_pool + gp.tm, tn), F32), pltpu.VMEM((gs.hdr_pool + gs.tm, tn), F32)],
        compiler_params=_params(),
        name="mix_b",
    )(hp, hs, w_in, w_grp, scale, ctxp, ctxs)


def _fold_lanes(v):
    out = v[:, 0:LANES]
    for c in range(1, v.shape[1] // LANES):
        out = out + v[:, c * LANES:(c + 1) * LANES]
    return out


def _emit_normed(x_new, gain, xg_ref):
    xg_ref[...] = (x_new * gain).astype(xg_ref.dtype)
    return _fold_lanes(x_new * x_new)


def _res_mm_kernel(*refs, n_lhs):
    m = pl.program_id(1)
    xp_ref, xs_ref = refs[0], refs[1]
    ap = refs[2:2 + n_lhs]
    a_s = refs[2 + n_lhs:2 + 2 * n_lhs]
    w = refs[2 + 2 * n_lhs:2 + 3 * n_lhs]
    gain = refs[2 + 3 * n_lhs][...]
    op_ref, os_ref, xgp_ref, xgs_ref, ssqp_ref, ssqs_ref = refs[3 + 3 * n_lhs:9 + 3 * n_lhs]
    wb = refs[9 + 3 * n_lhs:]

    def rows(x_ref, a_refs, o_ref, xg_ref, ssq_ref):
        acc = x_ref[...]
        for a_ref, wb_ref in zip(a_refs, wb):
            acc = acc + _dot(a_ref[...], wb_ref[...])
        o_ref[...] = acc
        ssq_ref[...] = _emit_normed(acc, gain, xg_ref)

    @pl.when(m == 0)
    def _():
        _round_weights(list(zip(w, wb)))
        rows(xs_ref, a_s, os_ref, xgs_ref, ssqs_ref)

    rows(xp_ref, ap, op_ref, xgp_ref, ssqp_ref)


def _res_mm(xp, xs, lhs_p, lhs_s, w, gain, gp, gs, name):
    d_out = xp.shape[1]
    n_lhs = len(lhs_p)
    kdim = lhs_p[0].shape[1]
    tn = TN_WIDE
    nn = d_out // tn
    xspec = [pl.BlockSpec((gp.tm, tn), lambda n, m: (m, n)),
             pl.BlockSpec((gs.tm, tn), lambda n, m: (0, n))]
    ssq_spec = [pl.BlockSpec((None, gp.tm, LANES), lambda n, m: (n, m, 0)),
                pl.BlockSpec((None, gs.tm, LANES), lambda n, m: (n, 0, 0))]
    lhs_specs = _lhs_specs(gp, gs, kdim)
    in_specs = xspec + [lhs_specs[0]] * n_lhs + [lhs_specs[1]] * n_lhs
    in_specs += [pl.BlockSpec((kdim, tn), lambda n, m, i=i: (i, n)) for i in range(n_lhs)]
    in_specs += [pl.BlockSpec((1, tn), lambda n, m: (0, n))]
    return pl.pallas_call(
        functools.partial(_res_mm_kernel, n_lhs=n_lhs),
        grid=(nn, gp.rows // gp.tm),
        in_specs=in_specs,
        out_specs=xspec + xspec + ssq_spec,
        out_shape=[jax.ShapeDtypeStruct(xp.shape, F32), jax.ShapeDtypeStruct(xs.shape, F32),
                   jax.ShapeDtypeStruct(xp.shape, BF16), jax.ShapeDtypeStruct(xs.shape, BF16),
                   jax.ShapeDtypeStruct((nn, gp.rows, LANES), F32),
                   jax.ShapeDtypeStruct((nn, gs.rows, LANES), F32)],
        scratch_shapes=[pltpu.VMEM((kdim, tn), BF16)] * n_lhs,
        compiler_params=_params(),
        name=name,
    )(xp, xs, *lhs_p, *lhs_s, *([w] * n_lhs), gain.reshape(1, d_out))


def _down_kernel(x_ref, a_ref, w_ref, o_ref):
    o_ref[...] = x_ref[...] + _dot(a_ref[...], w_ref[...])


def _down_norm_kernel(x_ref, a_ref, w_ref, g_ref, o_ref, xg_ref, ssq_ref):
    n = pl.program_id(1)
    acc = x_ref[...] + _dot(a_ref[...], w_ref[...])
    o_ref[...] = acc
    part = _emit_normed(acc, g_ref[...], xg_ref)

    @pl.when(n == 0)
    def _():
        ssq_ref[...] = part

    @pl.when(n != 0)
    def _():
        ssq_ref[...] += part


def _ffn_down(x, act, w, layer, gain=None, *, tm):
    rows, d_out = x.shape
    kdim = act.shape[1]
    tn = TN_WIDE
    xspec = pl.BlockSpec((tm, tn), lambda m, n: (m, n))
    in_specs = [xspec,
                pl.BlockSpec((tm, kdim), lambda m, n: (m, 0)),
                pl.BlockSpec((None, kdim, tn), lambda m, n: (layer, 0, n))]
    if gain is None:
        return pl.pallas_call(
            _down_kernel,
            grid=(rows // tm, d_out // tn),
            in_specs=in_specs,
            out_specs=xspec,
            out_shape=jax.ShapeDtypeStruct((rows, d_out), F32),
            compiler_params=_params(),
            name="ffn_down",
        )(x, act, w)
    return pl.pallas_call(
        _down_norm_kernel,
        grid=(rows // tm, d_out // tn),
        in_specs=in_specs + [pl.BlockSpec((1, tn), lambda m, n: (0, n))],
        out_specs=[xspec, xspec, pl.BlockSpec((None, tm, LANES), lambda m, n: (0, m, 0))],
        out_shape=[jax.ShapeDtypeStruct((rows, d_out), F32), jax.ShapeDtypeStruct((rows, d_out), BF16),
                   jax.ShapeDtypeStruct((1, rows, LANES), F32)],
        compiler_params=_params(),
        name="ffn_down_norm",
    )(x, act, w, gain.reshape(1, d_out))


def _rinv_kernel(ssq_ref, r_ref):
    tot = jnp.sum(jnp.sum(ssq_ref[...], axis=0), axis=-1, keepdims=True)
    r_ref[...] = jnp.broadcast_to(lax.rsqrt(tot * (1.0 / D_MODEL) + EPS), r_ref.shape)


def _rinv(ssq, tr):
    parts, rows, _ = ssq.shape
    return pl.pallas_call(
        _rinv_kernel,
        grid=(rows // tr,),
        in_specs=[pl.BlockSpec((parts, tr, LANES), lambda i: (0, i, 0))],
        out_specs=pl.BlockSpec((tr, LANES), lambda i: (i, 0)),
        out_shape=jax.ShapeDtypeStruct((rows, LANES), F32),
        compiler_params=_params(1),
        name="rinv",
    )(ssq)


def _rope_table_kernel(inv_ref, cos_ref, sin_ref, *, pos0, shift):
    rows = cos_ref.shape[0]
    row = lax.broadcasted_iota(jnp.int32, (rows, LANES), 0)
    lane = lax.broadcasted_iota(jnp.int32, (rows, LANES), 1)
    pos = (pos0 + _div_pow2(row, shift)).astype(F32)
    ang = pos * inv_ref[...]
    first_half = (lane & (HEAD_DIM - 1)) < (HEAD_DIM // 2)
    cos_ref[...] = jnp.cos(ang)
    sin = jnp.sin(ang)
    sin_ref[...] = jnp.where(first_half, -sin, sin)


def _rope_table(rows, pos0, shift):
    half = HEAD_DIM // 2
    inv = ROPE_THETA ** (-jnp.arange(half, dtype=F32) / half)
    inv = jnp.tile(inv, LANES // half).reshape(1, LANES)
    return pl.pallas_call(
        functools.partial(_rope_table_kernel, pos0=pos0, shift=shift),
        out_shape=[jax.ShapeDtypeStruct((rows, LANES), F32)] * 2,
        name="rope_table",
    )(inv)


def _rope(x, cos_ref, sin_ref, r0):
    rows, tn = x.shape
    reps = tn // LANES
    cos = jnp.concatenate([cos_ref[r0:r0 + rows, :]] * reps, axis=1)
    sin = jnp.concatenate([sin_ref[r0:r0 + rows, :]] * reps, axis=1)
    lane = lax.broadcasted_iota(jnp.int32, x.shape, 1)
    half = HEAD_DIM // 2
    first_half = (lane & (HEAD_DIM - 1)) < half
    partner = jnp.where(first_half, pltpu.roll(x, tn - half, 1), pltpu.roll(x, half, 1))
    return x * cos + partner * sin


def _dup_heads(x):
    lane = lax.broadcasted_iota(jnp.int32, (x.shape[0], LANES), 1)
    lower = lane < HEAD_DIM
    out = []
    for c in range(x.shape[1] // LANES):
        v = x[:, c * LANES:(c + 1) * LANES]
        r = pltpu.roll(v, HEAD_DIM, 1)
        out += [jnp.where(lower, v, r), jnp.where(lower, r, v)]
    return jnp.concatenate(out, axis=1)


def _row_scale(r_ref, r0, rows, cols):
    return jnp.concatenate([r_ref[r0:r0 + rows, :]] * (cols // LANES), axis=1)


def _proj_rows(h_ref, r_ref, w, bias, r0, rows):
    acc = _dot(h_ref[r0:r0 + rows, :], w[...])
    return acc * _row_scale(r_ref, r0, rows, acc.shape[1]) + bias


def _q_rows(h_ref, r_ref, cos_ref, sin_ref, q_ref, wq, bias, *, g):
    for r0 in range(0, g.tm, g.chunk):
        roped = _rope(_proj_rows(h_ref, r_ref, wq, bias, r0, g.chunk), cos_ref, sin_ref, r0)
        q_ref[r0:r0 + g.chunk, :] = (roped * (HEAD_DIM ** -0.5)).astype(q_ref.dtype)


def _q_kernel(hp_ref, hs_ref, rp_ref, rs_ref, w_ref, b_ref, cosp_ref, sinp_ref, coss_ref, sins_ref,
              qp_ref, qs_ref, wb_ref, *, gp, gs):
    m = pl.program_id(1)
    bias = b_ref[...]

    @pl.when(m == 0)
    def _():
        _round_weights([(w_ref, wb_ref)])
        _q_rows(hs_ref, rs_ref, coss_ref, sins_ref, qs_ref, wb_ref, bias, g=gs)

    _q_rows(hp_ref, rp_ref, cosp_ref, sinp_ref, qp_ref, wb_ref, bias, g=gp)


def _kv_rows(h_ref, r_ref, cos_ref, sin_ref, kv_ref, wkv, bias, is_key, *, g):
    for r0 in range(0, g.tm, g.chunk):
        acc = _proj_rows(h_ref, r_ref, wkv, bias, r0, g.chunk)
        roped = _rope(acc, cos_ref, sin_ref, r0)
        kv_ref[r0:r0 + g.chunk, :] = _dup_heads(jnp.where(is_key, roped, acc))


def _kv_kernel(hp_ref, hs_ref, rp_ref, rs_ref, w_ref, b_ref, cosp_ref, sinp_ref, coss_ref, sins_ref,
               kvp_ref, kvs_ref, wb_ref, *, gp, gs):
    is_key = pl.program_id(0) == 0
    m = pl.program_id(1)
    bias = b_ref[...]

    @pl.when(m == 0)
    def _():
        _round_weights([(w_ref, wb_ref)])
        _kv_rows(hs_ref, rs_ref, coss_ref, sins_ref, kvs_ref, wb_ref, bias, is_key, g=gs)

    _kv_rows(hp_ref, rp_ref, cosp_ref, sinp_ref, kvp_ref, wb_ref, bias, is_key, g=gp)


def _scale_specs(gp, gs):
    return [pl.BlockSpec((gp.tm, LANES), lambda n, m: (m, 0)),
            pl.BlockSpec((gs.tm, LANES), lambda n, m: (0, 0))]


def _qkv(hp, hs, rp, rs, w_qkv, b_qkv, tabp, tabs, gp, gs):
    tn = TN_WIDE
    nm = gp.rows // gp.tm
    tab_specs = [pl.BlockSpec((gp.tm, LANES), lambda n, m: (m % gp.tps, 0))] * 2
    tab_specs += [pl.BlockSpec((gs.tm, LANES), lambda n, m: (0, 0))] * 2

    def call(kern, n_tiles, col_off, out_tn, out_cols, dtype, name):
        return pl.pallas_call(
            functools.partial(kern, gp=gp, gs=gs),
            grid=(n_tiles, nm),
            in_specs=_lhs_specs(gp, gs, D_MODEL) + _scale_specs(gp, gs) + [
                pl.BlockSpec((D_MODEL, tn), lambda n, m: (0, n + col_off)),
                pl.BlockSpec((1, tn), lambda n, m: (0, n + col_off))] + tab_specs,
            out_specs=[pl.BlockSpec((gp.tm, out_tn), lambda n, m: (m, n)),
                       pl.BlockSpec((gs.tm, out_tn), lambda n, m: (0, n))],
            out_shape=[jax.ShapeDtypeStruct((gp.rows, out_cols), dtype),
                       jax.ShapeDtypeStruct((gs.rows, out_cols), dtype)],
            scratch_shapes=[pltpu.VMEM((D_MODEL, tn), BF16)],
            compiler_params=_params(),
            name=name,
        )(hp, hs, rp, rs, w_qkv, b_qkv, *tabp, *tabs)

    assert KV_COLS == tn
    qp, qs = call(_q_kernel, Q_COLS // tn, 0, tn, Q_COLS, BF16, "q_rope")
    kvp, kvs = call(_kv_kernel, 2, Q_COLS // tn, KV_DUP, 2 * KV_DUP, F32, "kv_rope")
    return qp, qs, kvp, kvs


def _block_diag(x2):
    lane = lax.broadcasted_iota(jnp.int32, x2.shape, 1)
    lo = jnp.where(lane < HEAD_DIM, x2, 0.0)
    hi = jnp.where(lane >= HEAD_DIM, x2, 0.0)
    return jnp.concatenate([lo, hi], axis=0).astype(BF16)


def _attn_kernel(q_ref, kvp_ref, kvc_ref, sink_ref, o_ref, *, blocks_per_seq):
    i = pl.program_id(0)
    pairs = GQA_GROUP // 2
    rows = pairs * WINDOW
    prev_bias = jnp.where((i % blocks_per_seq) > 0, 0.0, -jnp.inf)
    qi = lax.broadcasted_iota(jnp.int32, (rows, 2 * WINDOW), 0) & (WINDOW - 1)
    kj = lax.broadcasted_iota(jnp.int32, (rows, 2 * WINDOW), 1) & (WINDOW - 1)
    cur = kj <= qi
    lower = lax.broadcasted_iota(jnp.int32, (rows, LANES), 1) < HEAD_DIM
    ones_bd = _block_diag(jnp.ones((WINDOW, LANES), F32))
    nt = (((1,), (1,)), ((), ()))
    def qcols(kh):
        return [slice((kh * pairs + pi) * LANES, (kh * pairs + pi + 1) * LANES) for pi in range(pairs)]

    def scores(kh):
        kc = slice(kh * LANES, (kh + 1) * LANES)
        q = jnp.concatenate([q_ref[:, qc] for qc in qcols(kh)], axis=0)
        s_cur = lax.dot_general(q, _block_diag(kvc_ref[:, kc]), nt, preferred_element_type=F32)
        s_prev = lax.dot_general(q, _block_diag(kvp_ref[:, kc]), nt, preferred_element_type=F32)
        return jnp.where(cur, s_cur, s_prev + prev_bias)

    def finish(kh, s):
        vc = slice(KV_DUP + kh * LANES, KV_DUP + (kh + 1) * LANES)
        sink = sink_ref[kh]
        ps, es = [], []
        for hd in range(2):
            sh = s[:, hd * WINDOW:(hd + 1) * WINDOW]
            sk = sink[:, hd * HEAD_DIM:hd * HEAD_DIM + 1]
            mx = jnp.maximum(jnp.max(sh, axis=-1, keepdims=True), sk)
            ps.append(jnp.exp(sh - mx))
            es.append(jnp.exp(sk - mx))
        p = jnp.concatenate(ps, axis=1).astype(BF16)
        zero = jnp.zeros_like(p)
        o = _dot(jnp.where(cur, p, zero), _block_diag(kvc_ref[:, vc]))
        o = o + _dot(jnp.where(cur, zero, p), _block_diag(kvp_ref[:, vc]))
        den = _dot(p, ones_bd) + jnp.where(lower, es[0], es[1])
        o = (o / den).astype(o_ref.dtype)
        for pi, qc in enumerate(qcols(kh)):
            o_ref[:, qc] = o[pi * WINDOW:(pi + 1) * WINDOW, :]

    ahead = 8
    pending = [scores(kh) for kh in range(ahead)]
    for kh in range(N_KV_HEADS):
        if kh + ahead < N_KV_HEADS:
            pending.append(scores(kh + ahead))
        finish(kh, pending.pop(0))


def _attn_prompt(q, kv, sink_t):
    rows = q.shape[0]
    bps = SEQ // WINDOW
    return pl.pallas_call(
        functools.partial(_attn_kernel, blocks_per_seq=bps),
        grid=(rows // WINDOW,),
        in_specs=[
            pl.BlockSpec((WINDOW, Q_COLS), lambda i: (i, 0)),
            pl.BlockSpec((WINDOW, 2 * KV_DUP), lambda i: (jnp.where(i % bps == 0, i, i - 1), 0)),
            pl.BlockSpec((WINDOW, 2 * KV_DUP), lambda i: (i, 0)),
            pl.BlockSpec(sink_t.shape, lambda i: (0, 0, 0)),
        ],
        out_specs=pl.BlockSpec((WINDOW, Q_COLS), lambda i: (i, 0)),
        out_shape=jax.ShapeDtypeStruct((rows, Q_COLS), BF16),
        compiler_params=_params(1),
        name="attn_banded",
    )(q, kv, kv, sink_t)


def _attn_sample_kernel(q_ref, k_ref, v_ref, sink_ref, o_ref, *, n_keys, kpos0):
    rows = GQA_GROUP * DEC_SEQ
    keys = k_ref.shape[2]
    t = lax.broadcasted_iota(jnp.int32, (rows, keys), 0) % DEC_SEQ
    j = lax.broadcasted_iota(jnp.int32, (rows, keys), 1)
    mask = (j > t) & (j <= t + WINDOW) & (j < n_keys) & (j + kpos0 >= 0)
    scores = [lax.dot_general(q_ref[0, kh], k_ref[0, kh].astype(BF16), (((1,), (1,)), ((), ())),
                              preferred_element_type=F32) for kh in range(N_KV_HEADS)]
    for kh in range(N_KV_HEADS):
        s = jnp.where(mask, scores[kh], -jnp.inf)
        sk = sink_ref[kh][:, 0:1]
        mx = jnp.maximum(jnp.max(s, axis=-1, keepdims=True), sk)
        p = jnp.exp(s - mx)
        d = jnp.sum(p, axis=-1, keepdims=True) + jnp.exp(sk - mx)
        o = _dot(p.astype(BF16), v_ref[0, kh].astype(BF16))
        o_ref[0, kh] = (o / d).astype(o_ref.dtype)


def _attn_sample(q, k, v, sink_s, n_keys):
    nb, nkv, rows, d = q.shape
    keys = k.shape[2]
    return pl.pallas_call(
        functools.partial(_attn_sample_kernel, n_keys=n_keys, kpos0=PAST_LEN - WINDOW),
        grid=(nb,),
        in_specs=[
            pl.BlockSpec((1, nkv, rows, d), lambda b: (b, 0, 0, 0)),
            pl.BlockSpec((1, nkv, keys, d), lambda b: (b, 0, 0, 0)),
            pl.BlockSpec((1, nkv, keys, d), lambda b: (b, 0, 0, 0)),
            pl.BlockSpec(sink_s.shape, lambda b: (0, 0, 0)),
        ],
        out_specs=pl.BlockSpec((1, nkv, rows, d), lambda b: (b, 0, 0, 0)),
        out_shape=jax.ShapeDtypeStruct(q.shape, BF16),
        compiler_params=_params(1),
        name="attn_sample",
    )(q, k, v, sink_s)


def _ffn_up_rows(h_ref, r_ref, ctxg_ref, ctxu_ref, act_ref, tailg_ref, tailu_ref, z_ref, w, cw,
                 *, g, first):
    tm, hdr = g.tm, g.hdr_conv
    tn = act_ref.shape[1]
    if first is None:
        z_ref[0:hdr, 0:tn] = ctxg_ref[...]
        z_ref[0:hdr, tn:2 * tn] = ctxu_ref[...]
    else:
        @pl.when(first)
        def _():
            z_ref[0:hdr, 0:tn] = ctxg_ref[...]
            z_ref[0:hdr, tn:2 * tn] = ctxu_ref[...]

        @pl.when(jnp.logical_not(first))
        def _():
            z_ref[0:hdr, :] = z_ref[tm:tm + hdr, :]

    for r0 in range(0, tm, g.chunk):
        up = _dot(h_ref[r0:r0 + g.chunk, :], w[...])
        up = up * _row_scale(r_ref, r0, g.chunk, 2 * tn)
        lo = hdr + r0
        z_ref[lo:lo + g.chunk, :] = up
        y = _conv3(z_ref, cw, up, lo, g.shift, g.chunk)
        gate, lin = y[:, 0:tn], y[:, tn:2 * tn]
        silu = gate / (1.0 + jnp.exp(-gate))
        act_ref[r0:r0 + g.chunk, :] = (silu * lin).astype(act_ref.dtype)
    tailg_ref[...] = z_ref[tm:tm + hdr, 0:tn]
    tailu_ref[...] = z_ref[tm:tm + hdr, tn:2 * tn]


def _ffn_up_kernel(hp_ref, hs_ref, rp_ref, rs_ref, wg_ref, wu_ref, cwg_ref, cwu_ref,
                   ctxgp_ref, ctxup_ref, ctxgs_ref, ctxus_ref,
                   actp_ref, tailgp_ref, tailup_ref, acts_ref, tailgs_ref, tailus_ref,
                   wb_ref, zp_ref, zs_ref, *, gp, gs):
    m = pl.program_id(1)
    tn = wg_ref.shape[1]
    cw = jnp.concatenate([cwg_ref[...], cwu_ref[...]], axis=1)
    rows = functools.partial(_ffn_up_rows, w=wb_ref, cw=cw)

    @pl.when(m == 0)
    def _():
        wb_ref[:, 0:tn] = wg_ref[...].astype(BF16)
        wb_ref[:, tn:2 * tn] = wu_ref[...].astype(BF16)
        rows(hs_ref, rs_ref, ctxgs_ref, ctxus_ref, acts_ref, tailgs_ref, tailus_ref, zs_ref,
             g=gs, first=None)

    rows(hp_ref, rp_ref, ctxgp_ref, ctxup_ref, actp_ref, tailgp_ref, tailup_ref, zp_ref,
         g=gp, first=m % gp.tps == 0)


def _ffn_up(hp, hs, rp, rs, w_up, conv_w, ctxp, ctxs, layer, gp, gs):
    tn = TN_PAIR
    nn, nm = D_FF // tn, gp.rows // gp.tm
    ctxgp, outp = _group_specs(gp, gp.hdr_conv, tn)
    ctxup, _ = _group_specs(gp, gp.hdr_conv, tn, nn)
    ctxgs, outs = _single_specs(gs, gs.hdr_conv, tn)
    ctxus, _ = _single_specs(gs, gs.hdr_conv, tn, nn)
    lo = lambda n, m: (layer, 0, n)
    hi = lambda n, m: (layer, 0, n + nn)
    return pl.pallas_call(
        functools.partial(_ffn_up_kernel, gp=gp, gs=gs),
        grid=(nn, nm),
        in_specs=_lhs_specs(gp, gs, D_MODEL) + _scale_specs(gp, gs) + [
            pl.BlockSpec((None, D_MODEL, tn), lo), pl.BlockSpec((None, D_MODEL, tn), hi),
            pl.BlockSpec((None, 3, tn), lo), pl.BlockSpec((None, 3, tn), hi),
            ctxgp, ctxup, ctxgs, ctxus],
        out_specs=outp + outp[1:] + outs + outs[1:],
        out_shape=[
            jax.ShapeDtypeStruct((gp.rows, D_FF), BF16),
            jax.ShapeDtypeStruct((nm * gp.hdr_conv, D_FF), F32),
            jax.ShapeDtypeStruct((nm * gp.hdr_conv, D_FF), F32),
            jax.ShapeDtypeStruct((gs.rows, D_FF), BF16),
            jax.ShapeDtypeStruct((gs.hdr_conv, D_FF), F32),
            jax.ShapeDtypeStruct((gs.hdr_conv, D_FF), F32),
        ],
        scratch_shapes=[pltpu.VMEM((D_MODEL, 2 * tn), BF16),
                        pltpu.VMEM((gp.hdr_conv + gp.tm, 2 * tn), F32),
                        pltpu.VMEM((gs.hdr_conv + gs.tm, 2 * tn), F32)],
        compiler_params=_params(),
        name="ffn_up",
    )(hp, hs, rp, rs, w_up, w_up, conv_w, conv_w, ctxp, ctxp, ctxs, ctxs)


def _norm2(xp, xs, g, gp, gs, dtype=BF16):
    return _rmsnorm(xp, g, dtype, gp.tr), _rmsnorm(xs, g, dtype, gs.tr)


def _rinv2(ssqp, ssqs, gp, gs):
    return _rinv(ssqp, gp.tm), _rinv(ssqs, gs.tm)


def _ffn(xp, xs, normed, gp, gs, layer, w_up, conv_w, w_down_bf16, ctxp, ctxs, next_gain=None):
    hp, hs, rp, rs = normed
    actp, tgp, tup, acts, tgs, tus = _ffn_up(hp, hs, rp, rs, w_up, conv_w, ctxp, ctxs, layer, gp, gs)
    tails = (jnp.concatenate([tgp, tup], axis=1), jnp.concatenate([tgs, tus], axis=1))
    if next_gain is None:
        xp = _ffn_down(xp, actp, w_down_bf16, layer, tm=gp.tm_down)
        xs = _ffn_down(xs, acts, w_down_bf16, layer, tm=gs.tm_down)
        return xp, xs, None, tails
    xp, xgp, ssqp = _ffn_down(xp, actp, w_down_bf16, layer, next_gain, tm=gp.tm_down)
    xs, xgs, ssqs = _ffn_down(xs, acts, w_down_bf16, layer, next_gain, tm=gs.tm_down)
    return xp, xs, (xgp, xgs) + _rinv2(ssqp, ssqs, gp, gs), tails


def _undup_heads(kv_half):
    rows = kv_half.shape[0]
    return kv_half.reshape(rows, N_KV_HEADS, 2, HEAD_DIM)[:, :, 0, :].reshape(rows, KV_COLS)


def kernel(x_prompt, x_sample, state_conv_a, state_pool, cache_win_k, cache_win_v, state_ffn_conv,
           norm_mix, w_in_ab, conv_a, w_pool, pool_scale, w_out_ab,
           w_qkv, b_qkv, sinks, w_o, norm_ffn, w_up, conv_ffn, w_down, norm_final):
    bp = x_prompt.shape[0]
    nb = x_sample.shape[0]
    gp, gs = _prompt_group(bp), _sample_group()
    tiles = gp.rows // gp.tm
    last_tiles = slice(gp.tps - 1, tiles, gp.tps)

    def to_tm(a):
        return jnp.swapaxes(a, 0, 1).reshape(-1, a.shape[-1])

    def from_tm(a):
        return jnp.swapaxes(a.reshape(-1, nb, a.shape[-1]), 0, 1)

    def pad_hdr(a, hdr):
        return jnp.concatenate([jnp.zeros((hdr - a.shape[0], a.shape[1]), F32), a], axis=0)

    def last_rows(tail, hdr, keep):
        return tail.reshape(tiles, hdr, -1)[last_tiles, hdr - keep:]

    def last_steps(tail, keep):
        return from_tm(tail[tail.shape[0] - keep * nb:])

    sink = sinks[0].astype(F32)
    sink_t = jnp.broadcast_to(
        sink.reshape(N_KV_HEADS, GQA_GROUP // 2, 1, 2, 1),
        (N_KV_HEADS, GQA_GROUP // 2, WINDOW, 2, HEAD_DIM)).reshape(N_KV_HEADS, -1, LANES)
    sink_s = jnp.broadcast_to(
        jnp.repeat(sink.reshape(N_KV_HEADS, GQA_GROUP), DEC_SEQ, axis=1)[:, :, None],
        (N_KV_HEADS, GQA_GROUP * DEC_SEQ, LANES))

    xp = x_prompt.reshape(gp.rows, D_MODEL)
    xs = to_tm(x_sample)
    zeros_conv = lambda c: jnp.zeros((bp * gp.hdr_conv, c), F32)

    hp, hs = _norm2(xp, xs, norm_mix[0], gp, gs)
    yap, tail_ap, yas, tail_as = _mix_a(
        hp, hs, w_in_ab[0], conv_a[0], zeros_conv(D_A), to_tm(state_conv_a[0]), gp, gs)
    ybp, tail_pp, ybs, tail_ps = _mix_b(
        hp, hs, w_in_ab[0], w_pool[0], pool_scale[0].reshape(1, D_B),
        jnp.zeros((bp * gp.hdr_pool, D_B), F32), pad_hdr(to_tm(state_pool[0]), gs.hdr_pool), gp, gs)
    xp, xs, xgp, xgs, ssqp, ssqs = _res_mm(
        xp, xs, [yap, ybp], [yas, ybs], w_out_ab[0], norm_ffn[0], gp, gs, "out_proj")
    w_down_bf16 = w_down.astype(BF16)
    xp, xs, normed, (tail_f0p, tail_f0s) = _ffn(
        xp, xs, (xgp, xgs) + _rinv2(ssqp, ssqs, gp, gs), gp, gs, 0, w_up, conv_ffn, w_down_bf16,
        zeros_conv(2 * D_FF), to_tm(state_ffn_conv[0]), next_gain=norm_mix[1])

    tabp = _rope_table(gp.tm * gp.tps, gp.pos0, gp.shift)
    tabs = _rope_table(gs.tm, gs.pos0, gs.shift)
    qp, qs, kvp, kvs = _qkv(*normed, w_qkv[0], b_qkv[0].reshape(1, -1), tabp, tabs, gp, gs)

    op = _attn_prompt(qp, kvp, sink_t)
    kv_last = kvp.reshape(bp, SEQ, 2 * KV_DUP)[:, SEQ - WINDOW:].reshape(bp * WINDOW, 2 * KV_DUP)
    k_p = _undup_heads(kv_last[:, :KV_DUP]).reshape(bp, WINDOW, N_KV_HEADS, HEAD_DIM)
    v_p = _undup_heads(kv_last[:, KV_DUP:]).reshape(bp, WINDOW, N_KV_HEADS, HEAD_DIM)

    n_keys = WINDOW + DEC_SEQ
    kk = jnp.concatenate([cache_win_k[0].reshape(nb, WINDOW, KV_COLS),
                          from_tm(_undup_heads(kvs[:, :KV_DUP]))], axis=1)
    vv = jnp.concatenate([cache_win_v[0].reshape(nb, WINDOW, KV_COLS),
                          from_tm(_undup_heads(kvs[:, KV_DUP:]))], axis=1)

    def heads_major(a):
        a = jnp.pad(a, ((0, 0), (0, 2 * WINDOW - n_keys), (0, 0)))
        return a.reshape(nb, 2 * WINDOW, N_KV_HEADS, HEAD_DIM).transpose(0, 2, 1, 3)

    q5 = from_tm(qs).reshape(nb, DEC_SEQ, N_KV_HEADS, GQA_GROUP, HEAD_DIM)
    q5 = q5.transpose(0, 2, 3, 1, 4).reshape(nb, N_KV_HEADS, GQA_GROUP * DEC_SEQ, HEAD_DIM)
    o5 = _attn_sample(q5, heads_major(kk), heads_major(vv), sink_s, n_keys)
    o5 = o5.reshape(nb, N_KV_HEADS, GQA_GROUP, DEC_SEQ, HEAD_DIM).transpose(0, 3, 1, 2, 4)
    os_ = to_tm(o5.reshape(nb, DEC_SEQ, Q_COLS))
    k_s = kk[:, n_keys - WINDOW:].reshape(nb, WINDOW, N_KV_HEADS, HEAD_DIM)
    v_s = vv[:, n_keys - WINDOW:].reshape(nb, WINDOW, N_KV_HEADS, HEAD_DIM)

    xp, xs, xgp, xgs, ssqp, ssqs = _res_mm(
        xp, xs, [op], [os_], w_o[0], norm_ffn[1], gp, gs, "attn_out")
    xp, xs, _, (tail_f1p, tail_f1s) = _ffn(
        xp, xs, (xgp, xgs) + _rinv2(ssqp, ssqs, gp, gs), gp, gs, 1, w_up, conv_ffn, w_down_bf16,
        zeros_conv(2 * D_FF), to_tm(state_ffn_conv[1]))
    yp, ys = _norm2(xp, xs, norm_final, gp, gs, F32)

    ca_p = last_rows(tail_ap, gp.hdr_conv, 2)[None]
    pl_p = last_rows(tail_pp, gp.hdr_pool, POOL_CTX)[None]
    f_p = jnp.stack([last_rows(t, gp.hdr_conv, 2) for t in (tail_f0p, tail_f1p)])
    ca_s = last_steps(tail_as, 2)[None]
    pl_s = last_steps(tail_ps, POOL_CTX)[None]
    f_s = jnp.stack([last_steps(t, 2) for t in (tail_f0s, tail_f1s)])
    return (yp.reshape(bp, SEQ, D_MODEL), from_tm(ys), ca_p, ca_s, pl_p, pl_s,
            k_p[None], k_s[None], v_p[None], v_s[None], f_p, f_s)
```

```python
import functools
from typing import NamedTuple

import jax
import jax.numpy as jnp
from jax import lax
from jax.experimental import pallas as pl
from jax.experimental.pallas import tpu as pltpu

D_MODEL = 4096
SEQ = 2048
DEC_BATCH = 32
DEC_SEQ = 4
PAST_LEN = 16384
D_A = D_MODEL // 2
D_B = D_MODEL // 2
POOL_WINDOWS = (2, 4, 8, 16)
POOL_GROUP = D_B // len(POOL_WINDOWS)
POOL_CTX = max(POOL_WINDOWS) - 1
HEAD_DIM = 64
N_HEADS = D_MODEL // HEAD_DIM
N_KV_HEADS = 8
GQA_GROUP = N_HEADS // N_KV_HEADS
WINDOW = 128
ROPE_THETA = 10000.0
D_FF = 11008
EPS = 1e-6

F32 = jnp.float32
BF16 = jnp.bfloat16

LANES = 128
SUBLANES = 8
VMEM_LIMIT = 56 * 1024 * 1024
Q_COLS = N_HEADS * HEAD_DIM
KV_COLS = N_KV_HEADS * HEAD_DIM
KV_DUP = N_KV_HEADS * LANES
TN_WIDE = 512
TN_PAIR = 256


class _Group(NamedTuple):
    rows: int
    tm: int
    tps: int
    shift: int
    pos0: int
    chunk: int
    hdr_conv: int
    hdr_pool: int
    tm_down: int
    tr: int


def _prompt_group(batch, tm=1024):
    return _Group(rows=batch * SEQ, tm=tm, tps=SEQ // tm, shift=1, pos0=0, chunk=256,
                  hdr_conv=SUBLANES, hdr_pool=POOL_CTX + 1, tm_down=512, tr=512)


def _sample_group():
    rows = DEC_BATCH * DEC_SEQ
    return _Group(rows=rows, tm=rows, tps=1, shift=DEC_BATCH, pos0=PAST_LEN, chunk=rows,
                  hdr_conv=2 * DEC_BATCH, hdr_pool=(POOL_CTX + 1) * DEC_BATCH, tm_down=rows, tr=rows)


def _params(n_axes=2):
    return pltpu.CompilerParams(
        dimension_semantics=("arbitrary",) * n_axes, vmem_limit_bytes=VMEM_LIMIT)


def _dot(a, b):
    return jnp.dot(a, b, preferred_element_type=F32)


def _div_pow2(x, d):
    assert d & (d - 1) == 0
    return lax.shift_right_logical(x, d.bit_length() - 1)


def _round_weights(pairs):
    for src, dst in pairs:
        dst[...] = src[...].astype(BF16)


def _norm_kernel(x_ref, g_ref, o_ref):
    x = x_ref[...]
    y = x * lax.rsqrt(jnp.mean(x * x, axis=-1, keepdims=True) + EPS)
    o_ref[...] = (y * g_ref[...]).astype(o_ref.dtype)


def _rmsnorm(x, g, out_dtype, tr):
    rows, d = x.shape
    return pl.pallas_call(
        _norm_kernel,
        grid=(rows // tr,),
        in_specs=[pl.BlockSpec((tr, d), lambda i: (i, 0)),
                  pl.BlockSpec((1, d), lambda i: (0, 0))],
        out_specs=pl.BlockSpec((tr, d), lambda i: (i, 0)),
        out_shape=jax.ShapeDtypeStruct((rows, d), out_dtype),
        compiler_params=_params(1),
        name="rmsnorm",
    )(x, g.reshape(1, d))


def _set_header(z_ref, ctx_ref, first, tm, hdr):
    if first is None:
        z_ref[0:hdr, :] = ctx_ref[...]
        return

    @pl.when(first)
    def _():
        z_ref[0:hdr, :] = ctx_ref[...]

    @pl.when(jnp.logical_not(first))
    def _():
        z_ref[0:hdr, :] = z_ref[tm:tm + hdr, :]


def _conv3(z_ref, cw, cur, lo, shift, rows):
    y = z_ref[lo - 2 * shift:lo - 2 * shift + rows, :] * cw[0:1, :]
    y = y + z_ref[lo - shift:lo - shift + rows, :] * cw[1:2, :]
    return y + cur * cw[2:3, :]


def _group_specs(g, hdr, tn, col_off=0):
    ctx = pl.BlockSpec((hdr, tn), lambda n, m: (m // g.tps, n + col_off))
    outs = [pl.BlockSpec((g.tm, tn), lambda n, m: (m, n)),
            pl.BlockSpec((hdr, tn), lambda n, m: (m, n))]
    return ctx, outs


def _single_specs(g, hdr, tn, col_off=0):
    ctx = pl.BlockSpec((hdr, tn), lambda n, m: (0, n + col_off))
    outs = [pl.BlockSpec((g.tm, tn), lambda n, m: (0, n)),
            pl.BlockSpec((hdr, tn), lambda n, m: (0, n))]
    return ctx, outs


def _lhs_specs(gp, gs, k):
    return [pl.BlockSpec((gp.tm, k), lambda n, m: (m, 0)),
            pl.BlockSpec((gs.tm, k), lambda n, m: (0, 0))]


def _mix_a_rows(h_ref, ctx_ref, y_ref, tail_ref, z_ref, wb, wc, wx, cw, *, g, first):
    tm, hdr = g.tm, g.hdr_conv
    _set_header(z_ref, ctx_ref, first, tm, hdr)
    for r0 in range(0, tm, g.chunk):
        hc = h_ref[r0:r0 + g.chunk, :]
        gate_b = _dot(hc, wb[...])
        u = _dot(hc, wc[...]) * _dot(hc, wx[...])
        z_ref[hdr + r0:hdr + r0 + g.chunk, :] = u
        conv = _conv3(z_ref, cw, u, hdr + r0, g.shift, g.chunk)
        y_ref[r0:r0 + g.chunk, :] = (gate_b * conv).astype(y_ref.dtype)
    tail_ref[...] = z_ref[tm:tm + hdr, :]


def _mix_a_kernel(hp_ref, hs_ref, wb_ref, wc_ref, wx_ref, cw_ref, ctxp_ref, ctxs_ref,
                  yp_ref, tailp_ref, ys_ref, tails_ref,
                  wbb_ref, wcb_ref, wxb_ref, zp_ref, zs_ref, *, gp, gs):
    m = pl.program_id(1)
    cw = cw_ref[...]
    rows = functools.partial(_mix_a_rows, wb=wbb_ref, wc=wcb_ref, wx=wxb_ref, cw=cw)

    @pl.when(m == 0)
    def _():
        _round_weights([(wb_ref, wbb_ref), (wc_ref, wcb_ref), (wx_ref, wxb_ref)])
        rows(hs_ref, ctxs_ref, ys_ref, tails_ref, zs_ref, g=gs, first=None)

    rows(hp_ref, ctxp_ref, yp_ref, tailp_ref, zp_ref, g=gp, first=m % gp.tps == 0)


def _mix_a(hp, hs, w_in, conv_w, ctxp, ctxs, gp, gs):
    tn = TN_PAIR
    nn, nm = D_A // tn, gp.rows // gp.tm
    ctxp_spec, outp = _group_specs(gp, gp.hdr_conv, tn)
    ctxs_spec, outs = _single_specs(gs, gs.hdr_conv, tn)
    wspec = lambda off: pl.BlockSpec((D_MODEL, tn), lambda n, m: (0, n + off))
    return pl.pallas_call(
        functools.partial(_mix_a_kernel, gp=gp, gs=gs),
        grid=(nn, nm),
        in_specs=_lhs_specs(gp, gs, D_MODEL) + [
            wspec(0), wspec(nn), wspec(2 * nn),
            pl.BlockSpec((3, tn), lambda n, m: (0, n)),
            ctxp_spec, ctxs_spec],
        out_specs=outp + outs,
        out_shape=[
            jax.ShapeDtypeStruct((gp.rows, D_A), BF16),
            jax.ShapeDtypeStruct((nm * gp.hdr_conv, D_A), F32),
            jax.ShapeDtypeStruct((gs.rows, D_A), BF16),
            jax.ShapeDtypeStruct((gs.hdr_conv, D_A), F32),
        ],
        scratch_shapes=[pltpu.VMEM((D_MODEL, tn), BF16)] * 3 + [
            pltpu.VMEM((gp.hdr_conv + gp.tm, tn), F32), pltpu.VMEM((gs.hdr_conv + gs.tm, tn), F32)],
        compiler_params=_params(),
        name="mix_a",
    )(hp, hs, w_in, w_in, w_in, conv_w, ctxp, ctxs)


def _mix_b_rows(h_ref, ctx_ref, y_ref, tail_ref, z_ref, wp, wg, scale, *, g, first, tile, win):
    tm, hdr = g.tm, g.hdr_pool
    _set_header(z_ref, ctx_ref, first, tm, hdr)
    for r0 in range(0, tm, g.chunk):
        z_ref[hdr + r0:hdr + r0 + g.chunk, :] = _dot(h_ref[r0:r0 + g.chunk, :], wp[...])
    for r0 in range(0, tm, g.chunk):
        lo = hdr + r0
        p = z_ref[lo:lo + g.chunk, :]
        acc = p
        for i in range(1, win):
            acc = acc + z_ref[lo - i * g.shift:lo - i * g.shift + g.chunk, :]
        row = lax.broadcasted_iota(jnp.int32, (g.chunk, 1), 0) + (tile * tm + r0)
        pos = g.pos0 + _div_pow2(row, g.shift)
        cnt = jnp.minimum(pos + 1, win).astype(F32)
        pooled = acc / cnt - p
        yb = _dot(pooled.astype(BF16), wg[...]) * scale
        y_ref[r0:r0 + g.chunk, :] = yb.astype(y_ref.dtype)
    tail_ref[...] = z_ref[tm:tm + hdr, :]


def _mix_b_kernel(hp_ref, hs_ref, wp_ref, wg_ref, sc_ref, ctxp_ref, ctxs_ref,
                  yp_ref, tailp_ref, ys_ref, tails_ref,
                  wpb_ref, wgb_ref, zp_ref, zs_ref, *, gp, gs):
    grp = pl.program_id(0)
    m = pl.program_id(1)
    scale = sc_ref[...]

    @pl.when(m == 0)
    def _():
        _round_weights([(wp_ref, wpb_ref), (wg_ref, wgb_ref)])

    for gi, win in enumerate(POOL_WINDOWS):
        rows = functools.partial(_mix_b_rows, wp=wpb_ref, wg=wgb_ref, scale=scale, win=win)

        @pl.when(jnp.logical_and(grp == gi, m == 0))
        def _(rows=rows):
            rows(hs_ref, ctxs_ref, ys_ref, tails_ref, zs_ref, g=gs, first=None, tile=0)

        @pl.when(grp == gi)
        def _(rows=rows):
            rows(hp_ref, ctxp_ref, yp_ref, tailp_ref, zp_ref, g=gp,
                 first=m % gp.tps == 0, tile=m % gp.tps)


def _mix_b(hp, hs, w_in, w_grp, scale, ctxp, ctxs, gp, gs):
    tn, ng = POOL_GROUP, len(POOL_WINDOWS)
    nm = gp.rows // gp.tm
    col0 = 3 * D_A // tn
    ctxp_spec, outp = _group_specs(gp, gp.hdr_pool, tn)
    ctxs_spec, outs = _single_specs(gs, gs.hdr_pool, tn)
    return pl.pallas_call(
        functools.partial(_mix_b_kernel, gp=gp, gs=gs),
        grid=(ng, nm),
        in_specs=_lhs_specs(gp, gs, D_MODEL) + [
            pl.BlockSpec((D_MODEL, tn), lambda n, m: (0, col0 + n)),
            pl.BlockSpec((None, tn, tn), lambda n, m: (n, 0, 0)),
            pl.BlockSpec((1, tn), lambda n, m: (0, n)),
            ctxp_spec, ctxs_spec],
        out_specs=outp + outs,
        out_shape=[
            jax.ShapeDtypeStruct((gp.rows, D_B), BF16),
            jax.ShapeDtypeStruct((nm * gp.hdr_pool, D_B), F32),
            jax.ShapeDtypeStruct((gs.rows, D_B), BF16),
            jax.ShapeDtypeStruct((gs.hdr_pool, D_B), F32),
        ],
        scratch_shapes=[
            pltpu.VMEM((D_MODEL, tn), BF16), pltpu.VMEM((tn, tn), BF16),
            pltpu.VMEM((gp.hdr_pool + gp.tm, tn), F32), pltpu.VMEM((gs.hdr_pool + gs.tm, tn), F32)],
        compiler_params=_params(),
        name="mix_b",
    )(hp, hs, w_in, w_grp, scale, ctxp, ctxs)


def _res_mm_kernel(*refs, n_lhs):
    m = pl.program_id(1)
    xp_ref, xs_ref = refs[0], refs[1]
    ap = refs[2:2 + n_lhs]
    a_s = refs[2 + n_lhs:2 + 2 * n_lhs]
    w = refs[2 + 2 * n_lhs:2 + 3 * n_lhs]
    op_ref, os_ref = refs[2 + 3 * n_lhs], refs[3 + 3 * n_lhs]
    wb = refs[4 + 3 * n_lhs:]

    def rows(x_ref, a_refs, o_ref):
        acc = x_ref[...]
        for a_ref, wb_ref in zip(a_refs, wb):
            acc = acc + _dot(a_ref[...], wb_ref[...])
        o_ref[...] = acc

    @pl.when(m == 0)
    def _():
        _round_weights(list(zip(w, wb)))
        rows(xs_ref, a_s, os_ref)

    rows(xp_ref, ap, op_ref)


def _res_mm(xp, xs, lhs_p, lhs_s, w, gp, gs, name):
    d_out = xp.shape[1]
    n_lhs = len(lhs_p)
    kdim = lhs_p[0].shape[1]
    tn = TN_WIDE
    xspec = [pl.BlockSpec((gp.tm, tn), lambda n, m: (m, n)),
             pl.BlockSpec((gs.tm, tn), lambda n, m: (0, n))]
    lhs_specs = _lhs_specs(gp, gs, kdim)
    in_specs = xspec + [lhs_specs[0]] * n_lhs + [lhs_specs[1]] * n_lhs
    in_specs += [pl.BlockSpec((kdim, tn), lambda n, m, i=i: (i, n)) for i in range(n_lhs)]
    return pl.pallas_call(
        functools.partial(_res_mm_kernel, n_lhs=n_lhs),
        grid=(d_out // tn, gp.rows // gp.tm),
        in_specs=in_specs,
        out_specs=xspec,
        out_shape=[jax.ShapeDtypeStruct(xp.shape, F32), jax.ShapeDtypeStruct(xs.shape, F32)],
        scratch_shapes=[pltpu.VMEM((kdim, tn), BF16)] * n_lhs,
        compiler_params=_params(),
        name=name,
    )(xp, xs, *lhs_p, *lhs_s, *([w] * n_lhs))


def _down_kernel(x_ref, a_ref, w_ref, o_ref):
    o_ref[...] = x_ref[...] + _dot(a_ref[...], w_ref[...])


def _ffn_down(x, act, w, layer, *, tm):
    rows, d_out = x.shape
    kdim = act.shape[1]
    tn = TN_WIDE
    return pl.pallas_call(
        _down_kernel,
        grid=(rows // tm, d_out // tn),
        in_specs=[pl.BlockSpec((tm, tn), lambda m, n: (m, n)),
                  pl.BlockSpec((tm, kdim), lambda m, n: (m, 0)),
                  pl.BlockSpec((None, kdim, tn), lambda m, n: (layer, 0, n))],
        out_specs=pl.BlockSpec((tm, tn), lambda m, n: (m, n)),
        out_shape=jax.ShapeDtypeStruct((rows, d_out), F32),
        compiler_params=_params(),
        name="ffn_down",
    )(x, act, w)


def _rope_table_kernel(inv_ref, cos_ref, sin_ref, *, pos0, shift):
    rows = cos_ref.shape[0]
    row = lax.broadcasted_iota(jnp.int32, (rows, LANES), 0)
    lane = lax.broadcasted_iota(jnp.int32, (rows, LANES), 1)
    pos = (pos0 + _div_pow2(row, shift)).astype(F32)
    ang = pos * inv_ref[...]
    first_half = (lane & (HEAD_DIM - 1)) < (HEAD_DIM // 2)
    cos_ref[...] = jnp.cos(ang)
    sin = jnp.sin(ang)
    sin_ref[...] = jnp.where(first_half, -sin, sin)


def _rope_table(rows, pos0, shift):
    half = HEAD_DIM // 2
    inv = ROPE_THETA ** (-jnp.arange(half, dtype=F32) / half)
    inv = jnp.tile(inv, LANES // half).reshape(1, LANES)
    return pl.pallas_call(
        functools.partial(_rope_table_kernel, pos0=pos0, shift=shift),
        out_shape=[jax.ShapeDtypeStruct((rows, LANES), F32)] * 2,
        name="rope_table",
    )(inv)


def _rope(x, cos_ref, sin_ref, r0):
    rows, tn = x.shape
    reps = tn // LANES
    cos = jnp.concatenate([cos_ref[r0:r0 + rows, :]] * reps, axis=1)
    sin = jnp.concatenate([sin_ref[r0:r0 + rows, :]] * reps, axis=1)
    lane = lax.broadcasted_iota(jnp.int32, x.shape, 1)
    half = HEAD_DIM // 2
    first_half = (lane & (HEAD_DIM - 1)) < half
    partner = jnp.where(first_half, pltpu.roll(x, tn - half, 1), pltpu.roll(x, half, 1))
    return x * cos + partner * sin


def _dup_heads(x):
    lane = lax.broadcasted_iota(jnp.int32, (x.shape[0], LANES), 1)
    lower = lane < HEAD_DIM
    out = []
    for c in range(x.shape[1] // LANES):
        v = x[:, c * LANES:(c + 1) * LANES]
        r = pltpu.roll(v, HEAD_DIM, 1)
        out += [jnp.where(lower, v, r), jnp.where(lower, r, v)]
    return jnp.concatenate(out, axis=1)


def _q_rows(h_ref, cos_ref, sin_ref, q_ref, wq, bias, *, g):
    for r0 in range(0, g.tm, g.chunk):
        acc = _dot(h_ref[r0:r0 + g.chunk, :], wq[...]) + bias
        roped = _rope(acc, cos_ref, sin_ref, r0)
        q_ref[r0:r0 + g.chunk, :] = (roped * (HEAD_DIM ** -0.5)).astype(q_ref.dtype)


def _q_kernel(hp_ref, hs_ref, w_ref, b_ref, cosp_ref, sinp_ref, coss_ref, sins_ref,
              qp_ref, qs_ref, wb_ref, *, gp, gs):
    m = pl.program_id(1)
    bias = b_ref[...]

    @pl.when(m == 0)
    def _():
        _round_weights([(w_ref, wb_ref)])
        _q_rows(hs_ref, coss_ref, sins_ref, qs_ref, wb_ref, bias, g=gs)

    _q_rows(hp_ref, cosp_ref, sinp_ref, qp_ref, wb_ref, bias, g=gp)


def _kv_rows(h_ref, cos_ref, sin_ref, kv_ref, wkv, bias, is_key, *, g):
    for r0 in range(0, g.tm, g.chunk):
        acc = _dot(h_ref[r0:r0 + g.chunk, :], wkv[...]) + bias
        roped = _rope(acc, cos_ref, sin_ref, r0)
        kv_ref[r0:r0 + g.chunk, :] = _dup_heads(jnp.where(is_key, roped, acc))


def _kv_kernel(hp_ref, hs_ref, w_ref, b_ref, cosp_ref, sinp_ref, coss_ref, sins_ref,
               kvp_ref, kvs_ref, wb_ref, *, gp, gs):
    is_key = pl.program_id(0) == 0
    m = pl.program_id(1)
    bias = b_ref[...]

    @pl.when(m == 0)
    def _():
        _round_weights([(w_ref, wb_ref)])
        _kv_rows(hs_ref, coss_ref, sins_ref, kvs_ref, wb_ref, bias, is_key, g=gs)

    _kv_rows(hp_ref, cosp_ref, sinp_ref, kvp_ref, wb_ref, bias, is_key, g=gp)


def _qkv(hp, hs, w_qkv, b_qkv, tabp, tabs, gp, gs):
    tn = TN_WIDE
    nm = gp.rows // gp.tm
    tab_specs = [pl.BlockSpec((gp.tm, LANES), lambda n, m: (m % gp.tps, 0))] * 2
    tab_specs += [pl.BlockSpec((gs.tm, LANES), lambda n, m: (0, 0))] * 2

    def call(kern, n_tiles, col_off, out_tn, out_cols, dtype, name):
        return pl.pallas_call(
            functools.partial(kern, gp=gp, gs=gs),
            grid=(n_tiles, nm),
            in_specs=_lhs_specs(gp, gs, D_MODEL) + [
                pl.BlockSpec((D_MODEL, tn), lambda n, m: (0, n + col_off)),
                pl.BlockSpec((1, tn), lambda n, m: (0, n + col_off))] + tab_specs,
            out_specs=[pl.BlockSpec((gp.tm, out_tn), lambda n, m: (m, n)),
                       pl.BlockSpec((gs.tm, out_tn), lambda n, m: (0, n))],
            out_shape=[jax.ShapeDtypeStruct((gp.rows, out_cols), dtype),
                       jax.ShapeDtypeStruct((gs.rows, out_cols), dtype)],
            scratch_shapes=[pltpu.VMEM((D_MODEL, tn), BF16)],
            compiler_params=_params(),
            name=name,
        )(hp, hs, w_qkv, b_qkv, *tabp, *tabs)

    assert KV_COLS == tn
    qp, qs = call(_q_kernel, Q_COLS // tn, 0, tn, Q_COLS, BF16, "q_rope")
    kvp, kvs = call(_kv_kernel, 2, Q_COLS // tn, KV_DUP, 2 * KV_DUP, F32, "kv_rope")
    return qp, qs, kvp, kvs


def _block_diag(x2):
    lane = lax.broadcasted_iota(jnp.int32, x2.shape, 1)
    lo = jnp.where(lane < HEAD_DIM, x2, 0.0)
    hi = jnp.where(lane >= HEAD_DIM, x2, 0.0)
    return jnp.concatenate([lo, hi], axis=0).astype(BF16)


def _attn_kernel(q_ref, kvp_ref, kvc_ref, sink_ref, o_ref, *, blocks_per_seq):
    i = pl.program_id(0)
    pairs = GQA_GROUP // 2
    rows = pairs * WINDOW
    prev_bias = jnp.where((i % blocks_per_seq) > 0, 0.0, -jnp.inf)
    qi = lax.broadcasted_iota(jnp.int32, (rows, 2 * WINDOW), 0) & (WINDOW - 1)
    kj = lax.broadcasted_iota(jnp.int32, (rows, 2 * WINDOW), 1) & (WINDOW - 1)
    cur = kj <= qi
    lower = lax.broadcasted_iota(jnp.int32, (rows, LANES), 1) < HEAD_DIM
    ones_bd = _block_diag(jnp.ones((WINDOW, LANES), F32))
    nt = (((1,), (1,)), ((), ()))

    def qcols(kh):
        return [slice((kh * pairs + pi) * LANES, (kh * pairs + pi + 1) * LANES) for pi in range(pairs)]

    def scores(kh):
        kc = slice(kh * LANES, (kh + 1) * LANES)
        q = jnp.concatenate([q_ref[:, qc] for qc in qcols(kh)], axis=0)
        s_cur = lax.dot_general(q, _block_diag(kvc_ref[:, kc]), nt, preferred_element_type=F32)
        s_prev = lax.dot_general(q, _block_diag(kvp_ref[:, kc]), nt, preferred_element_type=F32)
        return jnp.where(cur, s_cur, s_prev + prev_bias)

    def finish(kh, s):
        vc = slice(KV_DUP + kh * LANES, KV_DUP + (kh + 1) * LANES)
        sink = sink_ref[kh]
        ps, es = [], []
        for hd in range(2):
            sh = s[:, hd * WINDOW:(hd + 1) * WINDOW]
            sk = sink[:, hd * HEAD_DIM:hd * HEAD_DIM + 1]
            mx = jnp.maximum(jnp.max(sh, axis=-1, keepdims=True), sk)
            ps.append(jnp.exp(sh - mx))
            es.append(jnp.exp(sk - mx))
        p = jnp.concatenate(ps, axis=1).astype(BF16)
        zero = jnp.zeros_like(p)
        o = _dot(jnp.where(cur, p, zero), _block_diag(kvc_ref[:, vc]))
        o = o + _dot(jnp.where(cur, zero, p), _block_diag(kvp_ref[:, vc]))
        den = _dot(p, ones_bd) + jnp.where(lower, es[0], es[1])
        o = (o / den).astype(o_ref.dtype)
        for pi, qc in enumerate(qcols(kh)):
            o_ref[:, qc] = o[pi * WINDOW:(pi + 1) * WINDOW, :]

    pending = [scores(kh) for kh in range(N_KV_HEADS)]
    for kh in range(N_KV_HEADS):
        finish(kh, pending[kh])


def _attn_prompt(q, kv, sink_t):
    rows = q.shape[0]
    bps = SEQ // WINDOW
    return pl.pallas_call(
        functools.partial(_attn_kernel, blocks_per_seq=bps),
        grid=(rows // WINDOW,),
        in_specs=[
            pl.BlockSpec((WINDOW, Q_COLS), lambda i: (i, 0)),
            pl.BlockSpec((WINDOW, 2 * KV_DUP), lambda i: (jnp.where(i % bps == 0, i, i - 1), 0)),
            pl.BlockSpec((WINDOW, 2 * KV_DUP), lambda i: (i, 0)),
            pl.BlockSpec(sink_t.shape, lambda i: (0, 0, 0)),
        ],
        out_specs=pl.BlockSpec((WINDOW, Q_COLS), lambda i: (i, 0)),
        out_shape=jax.ShapeDtypeStruct((rows, Q_COLS), BF16),
        compiler_params=_params(1),
        name="attn_banded",
    )(q, kv, kv, sink_t)


def _attn_sample_kernel(q_ref, k_ref, v_ref, sink_ref, o_ref, *, n_keys, kpos0):
    rows = GQA_GROUP * DEC_SEQ
    keys = k_ref.shape[2]
    t = lax.broadcasted_iota(jnp.int32, (rows, keys), 0) % DEC_SEQ
    j = lax.broadcasted_iota(jnp.int32, (rows, keys), 1)
    mask = (j > t) & (j <= t + WINDOW) & (j < n_keys) & (j + kpos0 >= 0)
    scores = [lax.dot_general(q_ref[0, kh], k_ref[0, kh].astype(BF16), (((1,), (1,)), ((), ())),
                              preferred_element_type=F32) for kh in range(N_KV_HEADS)]
    for kh in range(N_KV_HEADS):
        s = jnp.where(mask, scores[kh], -jnp.inf)
        sk = sink_ref[kh][:, 0:1]
        mx = jnp.maximum(jnp.max(s, axis=-1, keepdims=True), sk)
        p = jnp.exp(s - mx)
        d = jnp.sum(p, axis=-1, keepdims=True) + jnp.exp(sk - mx)
        o = _dot(p.astype(BF16), v_ref[0, kh].astype(BF16))
        o_ref[0, kh] = (o / d).astype(o_ref.dtype)


def _attn_sample(q, k, v, sink_s, n_keys):
    nb, nkv, rows, d = q.shape
    keys = k.shape[2]
    return pl.pallas_call(
        functools.partial(_attn_sample_kernel, n_keys=n_keys, kpos0=PAST_LEN - WINDOW),
        grid=(nb,),
        in_specs=[
            pl.BlockSpec((1, nkv, rows, d), lambda b: (b, 0, 0, 0)),
            pl.BlockSpec((1, nkv, keys, d), lambda b: (b, 0, 0, 0)),
            pl.BlockSpec((1, nkv, keys, d), lambda b: (b, 0, 0, 0)),
            pl.BlockSpec(sink_s.shape, lambda b: (0, 0, 0)),
        ],
        out_specs=pl.BlockSpec((1, nkv, rows, d), lambda b: (b, 0, 0, 0)),
        out_shape=jax.ShapeDtypeStruct(q.shape, BF16),
        compiler_params=_params(1),
        name="attn_sample",
    )(q, k, v, sink_s)


def _ffn_up_rows(h_ref, ctxg_ref, ctxu_ref, act_ref, tailg_ref, tailu_ref, z_ref, w, cw, *, g):
    assert g.tps == 1
    tm, hdr = g.tm, g.hdr_conv
    tn = act_ref.shape[1]
    z_ref[0:hdr, 0:tn] = ctxg_ref[...]
    z_ref[0:hdr, tn:2 * tn] = ctxu_ref[...]
    for r0 in range(0, tm, g.chunk):
        up = _dot(h_ref[r0:r0 + g.chunk, :], w[...])
        lo = hdr + r0
        z_ref[lo:lo + g.chunk, :] = up
        y = _conv3(z_ref, cw, up, lo, g.shift, g.chunk)
        gate, lin = y[:, 0:tn], y[:, tn:2 * tn]
        silu = gate / (1.0 + jnp.exp(-gate))
        act_ref[r0:r0 + g.chunk, :] = (silu * lin).astype(act_ref.dtype)
    tailg_ref[...] = z_ref[tm:tm + hdr, 0:tn]
    tailu_ref[...] = z_ref[tm:tm + hdr, tn:2 * tn]


def _ffn_up_kernel(hp_ref, hs_ref, w_hbm, cwg_ref, cwu_ref,
                   ctxgp_ref, ctxup_ref, ctxgs_ref, ctxus_ref,
                   actp_ref, tailgp_ref, tailup_ref, acts_ref, tailgs_ref, tailus_ref,
                   wstage_ref, wb_ref, zp_ref, zs_ref, sem, *, gp, gs, layer, n_tiles):
    n = pl.program_id(0)
    m = pl.program_id(1)
    tn = actp_ref.shape[1]
    cw = jnp.concatenate([cwg_ref[...], cwu_ref[...]], axis=1)
    rows = functools.partial(_ffn_up_rows, w=wb_ref, cw=cw)

    def weight_copies(col_tile):
        return [pltpu.make_async_copy(
            w_hbm.at[layer, :, pl.ds(half * D_FF + col_tile * tn, tn)],
            wstage_ref.at[:, pl.ds(half * tn, tn)], sem.at[half]) for half in range(2)]

    @pl.when(jnp.logical_and(n == 0, m == 0))
    def _():
        for cp in weight_copies(0):
            cp.start()

    @pl.when(m == 0)
    def _():
        for cp in weight_copies(n):
            cp.wait()
        wb_ref[...] = wstage_ref[...].astype(BF16)
        rows(hs_ref, ctxgs_ref, ctxus_ref, acts_ref, tailgs_ref, tailus_ref, zs_ref, g=gs)

    @pl.when(jnp.logical_and(m == 1, n + 1 < n_tiles))
    def _():
        for cp in weight_copies(n + 1):
            cp.start()

    rows(hp_ref, ctxgp_ref, ctxup_ref, actp_ref, tailgp_ref, tailup_ref, zp_ref, g=gp)


def _ffn_up(hp, hs, w_up, conv_w, ctxp, ctxs, layer, gp, gs):
    tn = TN_PAIR
    nn, nm = D_FF // tn, gp.rows // gp.tm
    assert nm >= 2
    ctxgp, outp = _group_specs(gp, gp.hdr_conv, tn)
    ctxup, _ = _group_specs(gp, gp.hdr_conv, tn, nn)
    ctxgs, outs = _single_specs(gs, gs.hdr_conv, tn)
    ctxus, _ = _single_specs(gs, gs.hdr_conv, tn, nn)
    lo = lambda n, m: (layer, 0, n)
    hi = lambda n, m: (layer, 0, n + nn)
    return pl.pallas_call(
        functools.partial(_ffn_up_kernel, gp=gp, gs=gs, layer=layer, n_tiles=nn),
        grid=(nn, nm),
        in_specs=_lhs_specs(gp, gs, D_MODEL) + [
            pl.BlockSpec(memory_space=pl.ANY),
            pl.BlockSpec((None, 3, tn), lo), pl.BlockSpec((None, 3, tn), hi),
            ctxgp, ctxup, ctxgs, ctxus],
        out_specs=outp + outp[1:] + outs + outs[1:],
        out_shape=[
            jax.ShapeDtypeStruct((gp.rows, D_FF), BF16),
            jax.ShapeDtypeStruct((nm * gp.hdr_conv, D_FF), F32),
            jax.ShapeDtypeStruct((nm * gp.hdr_conv, D_FF), F32),
            jax.ShapeDtypeStruct((gs.rows, D_FF), BF16),
            jax.ShapeDtypeStruct((gs.hdr_conv, D_FF), F32),
            jax.ShapeDtypeStruct((gs.hdr_conv, D_FF), F32),
        ],
        scratch_shapes=[pltpu.VMEM((D_MODEL, 2 * tn), F32),
                        pltpu.VMEM((D_MODEL, 2 * tn), BF16),
                        pltpu.VMEM((gp.hdr_conv + gp.tm, 2 * tn), F32),
                        pltpu.VMEM((gs.hdr_conv + gs.tm, 2 * tn), F32),
                        pltpu.SemaphoreType.DMA((2,))],
        compiler_params=_params(),
        name="ffn_up",
    )(hp, hs, w_up, conv_w, conv_w, ctxp, ctxp, ctxs, ctxs)


def _norm2(xp, xs, g, gp, gs, dtype=BF16):
    return _rmsnorm(xp, g, dtype, gp.tr), _rmsnorm(xs, g, dtype, gs.tr)


def _ffn(xp, xs, gp, gp_up, gs, layer, norm_g, w_up, conv_w, w_down_bf16, ctxp, ctxs):
    hp, hs = _norm2(xp, xs, norm_g, gp, gs)
    actp, tgp, tup, acts, tgs, tus = _ffn_up(hp, hs, w_up, conv_w, ctxp, ctxs, layer, gp_up, gs)
    xp = _ffn_down(xp, actp, w_down_bf16, layer, tm=gp.tm_down)
    xs = _ffn_down(xs, acts, w_down_bf16, layer, tm=gs.tm_down)
    return xp, xs, jnp.concatenate([tgp, tup], axis=1), jnp.concatenate([tgs, tus], axis=1)


def _undup_heads(kv_half):
    rows = kv_half.shape[0]
    return kv_half.reshape(rows, N_KV_HEADS, 2, HEAD_DIM)[:, :, 0, :].reshape(rows, KV_COLS)


def kernel(x_prompt, x_sample, state_conv_a, state_pool, cache_win_k, cache_win_v, state_ffn_conv,
           norm_mix, w_in_ab, conv_a, w_pool, pool_scale, w_out_ab,
           w_qkv, b_qkv, sinks, w_o, norm_ffn, w_up, conv_ffn, w_down, norm_final):
    bp = x_prompt.shape[0]
    nb = x_sample.shape[0]
    gp, gs = _prompt_group(bp), _sample_group()
    gp_up = _prompt_group(bp, tm=SEQ)

    def to_tm(a):
        return jnp.swapaxes(a, 0, 1).reshape(-1, a.shape[-1])

    def from_tm(a):
        return jnp.swapaxes(a.reshape(-1, nb, a.shape[-1]), 0, 1)

    def pad_hdr(a, hdr):
        return jnp.concatenate([jnp.zeros((hdr - a.shape[0], a.shape[1]), F32), a], axis=0)

    def last_rows(tail, g, hdr, keep):
        tiles = g.rows // g.tm
        return tail.reshape(tiles, hdr, -1)[g.tps - 1:tiles:g.tps, hdr - keep:]

    def last_steps(tail, keep):
        return from_tm(tail[tail.shape[0] - keep * nb:])

    sink = sinks[0].astype(F32)
    sink_t = jnp.broadcast_to(
        sink.reshape(N_KV_HEADS, GQA_GROUP // 2, 1, 2, 1),
        (N_KV_HEADS, GQA_GROUP // 2, WINDOW, 2, HEAD_DIM)).reshape(N_KV_HEADS, -1, LANES)
    sink_s = jnp.broadcast_to(
        jnp.repeat(sink.reshape(N_KV_HEADS, GQA_GROUP), DEC_SEQ, axis=1)[:, :, None],
        (N_KV_HEADS, GQA_GROUP * DEC_SEQ, LANES))

    xp = x_prompt.reshape(gp.rows, D_MODEL)
    xs = to_tm(x_sample)
    zeros_conv = lambda c: jnp.zeros((bp * gp.hdr_conv, c), F32)

    hp, hs = _norm2(xp, xs, norm_mix[0], gp, gs)
    yap, tail_ap, yas, tail_as = _mix_a(
        hp, hs, w_in_ab[0], conv_a[0], zeros_conv(D_A), to_tm(state_conv_a[0]), gp, gs)
    ybp, tail_pp, ybs, tail_ps = _mix_b(
        hp, hs, w_in_ab[0], w_pool[0], pool_scale[0].reshape(1, D_B),
        jnp.zeros((bp * gp.hdr_pool, D_B), F32), pad_hdr(to_tm(state_pool[0]), gs.hdr_pool), gp, gs)
    xp, xs = _res_mm(xp, xs, [yap, ybp], [yas, ybs], w_out_ab[0], gp, gs, "out_proj")
    w_down_bf16 = w_down.astype(BF16)
    xp, xs, tail_f0p, tail_f0s = _ffn(xp, xs, gp, gp_up, gs, 0, norm_ffn[0], w_up, conv_ffn, w_down_bf16,
                                      zeros_conv(2 * D_FF), to_tm(state_ffn_conv[0]))

    hp, hs = _norm2(xp, xs, norm_mix[1], gp, gs)
    tabp = _rope_table(gp.tm * gp.tps, gp.pos0, gp.shift)
    tabs = _rope_table(gs.tm, gs.pos0, gs.shift)
    qp, qs, kvp, kvs = _qkv(hp, hs, w_qkv[0], b_qkv[0].reshape(1, -1), tabp, tabs, gp, gs)

    op = _attn_prompt(qp, kvp, sink_t)
    kv_last = kvp.reshape(bp, SEQ, 2 * KV_DUP)[:, SEQ - WINDOW:].reshape(bp * WINDOW, 2 * KV_DUP)
    k_p = _undup_heads(kv_last[:, :KV_DUP]).reshape(bp, WINDOW, N_KV_HEADS, HEAD_DIM)
    v_p = _undup_heads(kv_last[:, KV_DUP:]).reshape(bp, WINDOW, N_KV_HEADS, HEAD_DIM)

    n_keys = WINDOW + DEC_SEQ
    kk = jnp.concatenate([cache_win_k[0].reshape(nb, WINDOW, KV_COLS),
                          from_tm(_undup_heads(kvs[:, :KV_DUP]))], axis=1)
    vv = jnp.concatenate([cache_win_v[0].reshape(nb, WINDOW, KV_COLS),
                          from_tm(_undup_heads(kvs[:, KV_DUP:]))], axis=1)

    def heads_major(a):
        a = jnp.pad(a, ((0, 0), (0, 2 * WINDOW - n_keys), (0, 0)))
        return a.reshape(nb, 2 * WINDOW, N_KV_HEADS, HEAD_DIM).transpose(0, 2, 1, 3)

    q5 = from_tm(qs).reshape(nb, DEC_SEQ, N_KV_HEADS, GQA_GROUP, HEAD_DIM)
    q5 = q5.transpose(0, 2, 3, 1, 4).reshape(nb, N_KV_HEADS, GQA_GROUP * DEC_SEQ, HEAD_DIM)
    o5 = _attn_sample(q5, heads_major(kk), heads_major(vv), sink_s, n_keys)
    o5 = o5.reshape(nb, N_KV_HEADS, GQA_GROUP, DEC_SEQ, HEAD_DIM).transpose(0, 3, 1, 2, 4)
    os_ = to_tm(o5.reshape(nb, DEC_SEQ, Q_COLS))
    k_s = kk[:, n_keys - WINDOW:].reshape(nb, WINDOW, N_KV_HEADS, HEAD_DIM)
    v_s = vv[:, n_keys - WINDOW:].reshape(nb, WINDOW, N_KV_HEADS, HEAD_DIM)

    xp, xs = _res_mm(xp, xs, [op], [os_], w_o[0], gp, gs, "attn_out")
    xp, xs, tail_f1p, tail_f1s = _ffn(xp, xs, gp, gp_up, gs, 1, norm_ffn[1], w_up, conv_ffn, w_down_bf16,
                                      zeros_conv(2 * D_FF), to_tm(state_ffn_conv[1]))
    yp, ys = _norm2(xp, xs, norm_final, gp, gs, F32)

    ca_p = last_rows(tail_ap, gp, gp.hdr_conv, 2)[None]
    pl_p = last_rows(tail_pp, gp, gp.hdr_pool, POOL_CTX)[None]
    f_p = jnp.stack([last_rows(t, gp_up, gp_up.hdr_conv, 2) for t in (tail_f0p, tail_f1p)])
    ca_s = last_steps(tail_as, 2)[None]
    pl_s = last_steps(tail_ps, POOL_CTX)[None]
    f_s = jnp.stack([last_steps(t, 2) for t in (tail_f0s, tail_f1s)])
    return (yp.reshape(bp, SEQ, D_MODEL), from_tm(ys), ca_p, ca_s, pl_p, pl_s,
            k_p[None], k_s[None], v_p[None], v_s[None], f_p, f_s)
```

```python
import functools
from typing import NamedTuple

import jax
import jax.numpy as jnp
from jax import lax
from jax.experimental import pallas as pl
from jax.experimental.pallas import tpu as pltpu

D_MODEL = 4096
SEQ = 2048
DEC_BATCH = 32
DEC_SEQ = 4
PAST_LEN = 16384
D_A = D_MODEL // 2
D_B = D_MODEL // 2
POOL_WINDOWS = (2, 4, 8, 16)
POOL_GROUP = D_B // len(POOL_WINDOWS)
POOL_CTX = max(POOL_WINDOWS) - 1
HEAD_DIM = 64
N_HEADS = D_MODEL // HEAD_DIM
N_KV_HEADS = 8
GQA_GROUP = N_HEADS // N_KV_HEADS
WINDOW = 128
ROPE_THETA = 10000.0
D_FF = 11008
EPS = 1e-6

F32 = jnp.float32
BF16 = jnp.bfloat16

LANES = 128
SUBLANES = 8
VMEM_LIMIT = 56 * 1024 * 1024
Q_COLS = N_HEADS * HEAD_DIM
KV_COLS = N_KV_HEADS * HEAD_DIM
KV_DUP = N_KV_HEADS * LANES
TN_WIDE = 512
TN_PAIR = 256


class _Group(NamedTuple):
    rows: int
    tm: int
    tps: int
    shift: int
    pos0: int
    chunk: int
    hdr_conv: int
    hdr_pool: int
    tm_down: int
    tr: int


def _prompt_group(batch, tm=1024):
    return _Group(rows=batch * SEQ, tm=tm, tps=SEQ // tm, shift=1, pos0=0, chunk=256,
                  hdr_conv=SUBLANES, hdr_pool=POOL_CTX + 1, tm_down=512, tr=512)


def _sample_group():
    rows = DEC_BATCH * DEC_SEQ
    return _Group(rows=rows, tm=rows, tps=1, shift=DEC_BATCH, pos0=PAST_LEN, chunk=rows,
                  hdr_conv=2 * DEC_BATCH, hdr_pool=(POOL_CTX + 1) * DEC_BATCH, tm_down=rows, tr=rows)


def _params(n_axes=2):
    return pltpu.CompilerParams(
        dimension_semantics=("arbitrary",) * n_axes, vmem_limit_bytes=VMEM_LIMIT)


def _dot(a, b):
    return jnp.dot(a, b, preferred_element_type=F32)


def _div_pow2(x, d):
    assert d & (d - 1) == 0
    return lax.shift_right_logical(x, d.bit_length() - 1)


def _round_weights(pairs):
    for src, dst in pairs:
        dst[...] = src[...].astype(BF16)


def _norm_kernel(x_ref, g_ref, o_ref):
    x = x_ref[...]
    y = x * lax.rsqrt(jnp.mean(x * x, axis=-1, keepdims=True) + EPS)
    o_ref[...] = (y * g_ref[...]).astype(o_ref.dtype)


def _rmsnorm(x, g, out_dtype, tr):
    rows, d = x.shape
    return pl.pallas_call(
        _norm_kernel,
        grid=(rows // tr,),
        in_specs=[pl.BlockSpec((tr, d), lambda i: (i, 0)),
                  pl.BlockSpec((1, d), lambda i: (0, 0))],
        out_specs=pl.BlockSpec((tr, d), lambda i: (i, 0)),
        out_shape=jax.ShapeDtypeStruct((rows, d), out_dtype),
        compiler_params=_params(1),
        name="rmsnorm",
    )(x, g.reshape(1, d))


def _set_header(z_ref, ctx_ref, first, tm, hdr):
    if first is None:
        z_ref[0:hdr, :] = ctx_ref[...]
        return

    @pl.when(first)
    def _():
        z_ref[0:hdr, :] = ctx_ref[...]

    @pl.when(jnp.logical_not(first))
    def _():
        z_ref[0:hdr, :] = z_ref[tm:tm + hdr, :]


def _conv3(z_ref, cw, cur, lo, shift, rows):
    y = z_ref[lo - 2 * shift:lo - 2 * shift + rows, :] * cw[0:1, :]
    y = y + z_ref[lo - shift:lo - shift + rows, :] * cw[1:2, :]
    return y + cur * cw[2:3, :]


def _group_specs(g, hdr, tn, col_off=0):
    ctx = pl.BlockSpec((hdr, tn), lambda n, m: (m // g.tps, n + col_off))
    outs = [pl.BlockSpec((g.tm, tn), lambda n, m: (m, n)),
            pl.BlockSpec((hdr, tn), lambda n, m: (m, n))]
    return ctx, outs


def _single_specs(g, hdr, tn, col_off=0):
    ctx = pl.BlockSpec((hdr, tn), lambda n, m: (0, n + col_off))
    outs = [pl.BlockSpec((g.tm, tn), lambda n, m: (0, n)),
            pl.BlockSpec((hdr, tn), lambda n, m: (0, n))]
    return ctx, outs


def _lhs_specs(gp, gs, k):
    return [pl.BlockSpec((gp.tm, k), lambda n, m: (m, 0)),
            pl.BlockSpec((gs.tm, k), lambda n, m: (0, 0))]


def _mix_a_rows(h_ref, ctx_ref, y_ref, tail_ref, z_ref, wb, wc, wx, cw, *, g, first):
    tm, hdr = g.tm, g.hdr_conv
    _set_header(z_ref, ctx_ref, first, tm, hdr)
    for r0 in range(0, tm, g.chunk):
        hc = h_ref[r0:r0 + g.chunk, :]
        gate_b = _dot(hc, wb[...])
        u = _dot(hc, wc[...]) * _dot(hc, wx[...])
        z_ref[hdr + r0:hdr + r0 + g.chunk, :] = u
        conv = _conv3(z_ref, cw, u, hdr + r0, g.shift, g.chunk)
        y_ref[r0:r0 + g.chunk, :] = (gate_b * conv).astype(y_ref.dtype)
    tail_ref[...] = z_ref[tm:tm + hdr, :]


def _mix_a_kernel(hp_ref, hs_ref, wb_ref, wc_ref, wx_ref, cw_ref, ctxp_ref, ctxs_ref,
                  yp_ref, tailp_ref, ys_ref, tails_ref,
                  wbb_ref, wcb_ref, wxb_ref, zp_ref, zs_ref, *, gp, gs):
    m = pl.program_id(1)
    cw = cw_ref[...]
    rows = functools.partial(_mix_a_rows, wb=wbb_ref, wc=wcb_ref, wx=wxb_ref, cw=cw)

    @pl.when(m == 0)
    def _():
        _round_weights([(wb_ref, wbb_ref), (wc_ref, wcb_ref), (wx_ref, wxb_ref)])
        rows(hs_ref, ctxs_ref, ys_ref, tails_ref, zs_ref, g=gs, first=None)

    rows(hp_ref, ctxp_ref, yp_ref, tailp_ref, zp_ref, g=gp, first=m % gp.tps == 0)


def _mix_a(hp, hs, w_in, conv_w, ctxp, ctxs, gp, gs):
    tn = TN_PAIR
    nn, nm = D_A // tn, gp.rows // gp.tm
    ctxp_spec, outp = _group_specs(gp, gp.hdr_conv, tn)
    ctxs_spec, outs = _single_specs(gs, gs.hdr_conv, tn)
    wspec = lambda off: pl.BlockSpec((D_MODEL, tn), lambda n, m: (0, n + off))
    return pl.pallas_call(
        functools.partial(_mix_a_kernel, gp=gp, gs=gs),
        grid=(nn, nm),
        in_specs=_lhs_specs(gp, gs, D_MODEL) + [
            wspec(0), wspec(nn), wspec(2 * nn),
            pl.BlockSpec((3, tn), lambda n, m: (0, n)),
            ctxp_spec, ctxs_spec],
        out_specs=outp + outs,
        out_shape=[
            jax.ShapeDtypeStruct((gp.rows, D_A), BF16),
            jax.ShapeDtypeStruct((nm * gp.hdr_conv, D_A), F32),
            jax.ShapeDtypeStruct((gs.rows, D_A), BF16),
            jax.ShapeDtypeStruct((gs.hdr_conv, D_A), F32),
        ],
        scratch_shapes=[pltpu.VMEM((D_MODEL, tn), BF16)] * 3 + [
            pltpu.VMEM((gp.hdr_conv + gp.tm, tn), F32), pltpu.VMEM((gs.hdr_conv + gs.tm, tn), F32)],
        compiler_params=_params(),
        name="mix_a",
    )(hp, hs, w_in, w_in, w_in, conv_w, ctxp, ctxs)


def _mix_b_rows(h_ref, ctx_ref, y_ref, tail_ref, z_ref, wp, wg, scale, *, g, first, tile, win):
    tm, hdr = g.tm, g.hdr_pool
    _set_header(z_ref, ctx_ref, first, tm, hdr)
    for r0 in range(0, tm, g.chunk):
        z_ref[hdr + r0:hdr + r0 + g.chunk, :] = _dot(h_ref[r0:r0 + g.chunk, :], wp[...])
    for r0 in range(0, tm, g.chunk):
        lo = hdr + r0
        p = z_ref[lo:lo + g.chunk, :]
        acc = p
        for i in range(1, win):
            acc = acc + z_ref[lo - i * g.shift:lo - i * g.shift + g.chunk, :]
        row = lax.broadcasted_iota(jnp.int32, (g.chunk, 1), 0) + (tile * tm + r0)
        pos = g.pos0 + _div_pow2(row, g.shift)
        cnt = jnp.minimum(pos + 1, win).astype(F32)
        pooled = acc / cnt - p
        yb = _dot(pooled.astype(BF16), wg[...]) * scale
        y_ref[r0:r0 + g.chunk, :] = yb.astype(y_ref.dtype)
    tail_ref[...] = z_ref[tm:tm + hdr, :]


def _mix_b_kernel(hp_ref, hs_ref, wp_ref, wg_ref, sc_ref, ctxp_ref, ctxs_ref,
                  yp_ref, tailp_ref, ys_ref, tails_ref,
                  wpb_ref, wgb_ref, zp_ref, zs_ref, *, gp, gs):
    grp = pl.program_id(0)
    m = pl.program_id(1)
    scale = sc_ref[...]

    @pl.when(m == 0)
    def _():
        _round_weights([(wp_ref, wpb_ref), (wg_ref, wgb_ref)])

    for gi, win in enumerate(POOL_WINDOWS):
        rows = functools.partial(_mix_b_rows, wp=wpb_ref, wg=wgb_ref, scale=scale, win=win)

        @pl.when(jnp.logical_and(grp == gi, m == 0))
        def _(rows=rows):
            rows(hs_ref, ctxs_ref, ys_ref, tails_ref, zs_ref, g=gs, first=None, tile=0)

        @pl.when(grp == gi)
        def _(rows=rows):
            rows(hp_ref, ctxp_ref, yp_ref, tailp_ref, zp_ref, g=gp,
                 first=m % gp.tps == 0, tile=m % gp.tps)


def _mix_b(hp, hs, w_in, w_grp, scale, ctxp, ctxs, gp, gs):
    tn, ng = POOL_GROUP, len(POOL_WINDOWS)
    nm = gp.rows // gp.tm
    col0 = 3 * D_A // tn
    ctxp_spec, outp = _group_specs(gp, gp.hdr_pool, tn)
    ctxs_spec, outs = _single_specs(gs, gs.hdr_pool, tn)
    return pl.pallas_call(
        functools.partial(_mix_b_kernel, gp=gp, gs=gs),
        grid=(ng, nm),
        in_specs=_lhs_specs(gp, gs, D_MODEL) + [
            pl.BlockSpec((D_MODEL, tn), lambda n, m: (0, col0 + n)),
            pl.BlockSpec((None, tn, tn), lambda n, m: (n, 0, 0)),
            pl.BlockSpec((1, tn), lambda n, m: (0, n)),
            ctxp_spec, ctxs_spec],
        out_specs=outp + outs,
        out_shape=[
            jax.ShapeDtypeStruct((gp.rows, D_B), BF16),
            jax.ShapeDtypeStruct((nm * gp.hdr_pool, D_B), F32),
            jax.ShapeDtypeStruct((gs.rows, D_B), BF16),
            jax.ShapeDtypeStruct((gs.hdr_pool, D_B), F32),
        ],
        scratch_shapes=[
            pltpu.VMEM((D_MODEL, tn), BF16), pltpu.VMEM((tn, tn), BF16),
            pltpu.VMEM((gp.hdr_pool + gp.tm, tn), F32), pltpu.VMEM((gs.hdr_pool + gs.tm, tn), F32)],
        compiler_params=_params(),
        name="mix_b",
    )(hp, hs, w_in, w_grp, scale, ctxp, ctxs)


def _res_mm_kernel(*refs, n_lhs):
    m = pl.program_id(1)
    xp_ref, xs_ref = refs[0], refs[1]
    ap = refs[2:2 + n_lhs]
    a_s = refs[2 + n_lhs:2 + 2 * n_lhs]
    w = refs[2 + 2 * n_lhs:2 + 3 * n_lhs]
    op_ref, os_ref = refs[2 + 3 * n_lhs], refs[3 + 3 * n_lhs]
    wb = refs[4 + 3 * n_lhs:]

    def rows(x_ref, a_refs, o_ref):
        acc = x_ref[...]
        for a_ref, wb_ref in zip(a_refs, wb):
            acc = acc + _dot(a_ref[...], wb_ref[...])
        o_ref[...] = acc

    @pl.when(m == 0)
    def _():
        _round_weights(list(zip(w, wb)))
        rows(xs_ref, a_s, os_ref)

    rows(xp_ref, ap, op_ref)


def _res_mm(xp, xs, lhs_p, lhs_s, w, gp, gs, name):
    d_out = xp.shape[1]
    n_lhs = len(lhs_p)
    kdim = lhs_p[0].shape[1]
    tn = TN_WIDE
    xspec = [pl.BlockSpec((gp.tm, tn), lambda n, m: (m, n)),
             pl.BlockSpec((gs.tm, tn), lambda n, m: (0, n))]
    lhs_specs = _lhs_specs(gp, gs, kdim)
    in_specs = xspec + [lhs_specs[0]] * n_lhs + [lhs_specs[1]] * n_lhs
    in_specs += [pl.BlockSpec((kdim, tn), lambda n, m, i=i: (i, n)) for i in range(n_lhs)]
    return pl.pallas_call(
        functools.partial(_res_mm_kernel, n_lhs=n_lhs),
        grid=(d_out // tn, gp.rows // gp.tm),
        in_specs=in_specs,
        out_specs=xspec,
        out_shape=[jax.ShapeDtypeStruct(xp.shape, F32), jax.ShapeDtypeStruct(xs.shape, F32)],
        scratch_shapes=[pltpu.VMEM((kdim, tn), BF16)] * n_lhs,
        compiler_params=_params(),
        name=name,
    )(xp, xs, *lhs_p, *lhs_s, *([w] * n_lhs))


def _down_kernel(x_ref, a_ref, w_ref, o_ref):
    o_ref[...] = x_ref[...] + _dot(a_ref[...], w_ref[...])


def _ffn_down(x, act, w_bf16, *, tm):
    rows, d_out = x.shape
    kdim = act.shape[1]
    tn = TN_WIDE
    return pl.pallas_call(
        _down_kernel,
        grid=(rows // tm, d_out // tn),
        in_specs=[pl.BlockSpec((tm, tn), lambda m, n: (m, n)),
                  pl.BlockSpec((tm, kdim), lambda m, n: (m, 0)),
                  pl.BlockSpec((kdim, tn), lambda m, n: (0, n))],
        out_specs=pl.BlockSpec((tm, tn), lambda m, n: (m, n)),
        out_shape=jax.ShapeDtypeStruct((rows, d_out), F32),
        compiler_params=_params(),
        name="ffn_down",
    )(x, act, w_bf16)


def _down_round_kernel(x_ref, a_ref, w_ref, o_ref, wb_ref):
    wb = w_ref[...].astype(BF16)
    wb_ref[...] = wb
    o_ref[...] = x_ref[...] + _dot(a_ref[...], wb)


def _ffn_down_sample(x, act, w, layer):
    rows, d_out = x.shape
    kdim = act.shape[1]
    tn = TN_PAIR
    return pl.pallas_call(
        _down_round_kernel,
        grid=(d_out // tn,),
        in_specs=[pl.BlockSpec((rows, tn), lambda n: (0, n)),
                  pl.BlockSpec((rows, kdim), lambda n: (0, 0)),
                  pl.BlockSpec((None, kdim, tn), lambda n: (layer, 0, n))],
        out_specs=[pl.BlockSpec((rows, tn), lambda n: (0, n)),
                   pl.BlockSpec((kdim, tn), lambda n: (0, n))],
        out_shape=[jax.ShapeDtypeStruct((rows, d_out), F32),
                   jax.ShapeDtypeStruct((kdim, d_out), BF16)],
        compiler_params=_params(1),
        name="ffn_down_sample",
    )(x, act, w)


def _rope_table_kernel(inv_ref, cos_ref, sin_ref, *, pos0, shift):
    rows = cos_ref.shape[0]
    row = lax.broadcasted_iota(jnp.int32, (rows, LANES), 0)
    lane = lax.broadcasted_iota(jnp.int32, (rows, LANES), 1)
    pos = (pos0 + _div_pow2(row, shift)).astype(F32)
    ang = pos * inv_ref[...]
    first_half = (lane & (HEAD_DIM - 1)) < (HEAD_DIM // 2)
    cos_ref[...] = jnp.cos(ang)
    sin = jnp.sin(ang)
    sin_ref[...] = jnp.where(first_half, -sin, sin)


def _rope_table(rows, pos0, shift):
    half = HEAD_DIM // 2
    inv = ROPE_THETA ** (-jnp.arange(half, dtype=F32) / half)
    inv = jnp.tile(inv, LANES // half).reshape(1, LANES)
    return pl.pallas_call(
        functools.partial(_rope_table_kernel, pos0=pos0, shift=shift),
        out_shape=[jax.ShapeDtypeStruct((rows, LANES), F32)] * 2,
        name="rope_table",
    )(inv)


def _rope(x, cos_ref, sin_ref, r0):
    rows, tn = x.shape
    reps = tn // LANES
    cos = jnp.concatenate([cos_ref[r0:r0 + rows, :]] * reps, axis=1)
    sin = jnp.concatenate([sin_ref[r0:r0 + rows, :]] * reps, axis=1)
    lane = lax.broadcasted_iota(jnp.int32, x.shape, 1)
    half = HEAD_DIM // 2
    first_half = (lane & (HEAD_DIM - 1)) < half
    partner = jnp.where(first_half, pltpu.roll(x, tn - half, 1), pltpu.roll(x, half, 1))
    return x * cos + partner * sin


def _dup_heads(x):
    lane = lax.broadcasted_iota(jnp.int32, (x.shape[0], LANES), 1)
    lower = lane < HEAD_DIM
    out = []
    for c in range(x.shape[1] // LANES):
        v = x[:, c * LANES:(c + 1) * LANES]
        r = pltpu.roll(v, HEAD_DIM, 1)
        out += [jnp.where(lower, v, r), jnp.where(lower, r, v)]
    return jnp.concatenate(out, axis=1)


def _q_rows(h_ref, cos_ref, sin_ref, q_ref, wq, bias, *, g):
    for r0 in range(0, g.tm, g.chunk):
        acc = _dot(h_ref[r0:r0 + g.chunk, :], wq[...]) + bias
        roped = _rope(acc, cos_ref, sin_ref, r0)
        q_ref[r0:r0 + g.chunk, :] = (roped * (HEAD_DIM ** -0.5)).astype(q_ref.dtype)


def _q_kernel(hp_ref, hs_ref, w_ref, b_ref, cosp_ref, sinp_ref, coss_ref, sins_ref,
              qp_ref, qs_ref, wb_ref, *, gp, gs):
    m = pl.program_id(1)
    bias = b_ref[...]

    @pl.when(m == 0)
    def _():
        _round_weights([(w_ref, wb_ref)])
        _q_rows(hs_ref, coss_ref, sins_ref, qs_ref, wb_ref, bias, g=gs)

    _q_rows(hp_ref, cosp_ref, sinp_ref, qp_ref, wb_ref, bias, g=gp)


def _kv_rows(h_ref, cos_ref, sin_ref, kv_ref, wkv, bias, is_key, *, g):
    for r0 in range(0, g.tm, g.chunk):
        acc = _dot(h_ref[r0:r0 + g.chunk, :], wkv[...]) + bias
        roped = _rope(acc, cos_ref, sin_ref, r0)
        kv_ref[r0:r0 + g.chunk, :] = _dup_heads(jnp.where(is_key, roped, acc))


def _kv_kernel(hp_ref, hs_ref, w_ref, b_ref, cosp_ref, sinp_ref, coss_ref, sins_ref,
               kvp_ref, kvs_ref, wb_ref, *, gp, gs):
    is_key = pl.program_id(0) == 0
    m = pl.program_id(1)
    bias = b_ref[...]

    @pl.when(m == 0)
    def _():
        _round_weights([(w_ref, wb_ref)])
        _kv_rows(hs_ref, coss_ref, sins_ref, kvs_ref, wb_ref, bias, is_key, g=gs)

    _kv_rows(hp_ref, cosp_ref, sinp_ref, kvp_ref, wb_ref, bias, is_key, g=gp)


def _qkv(hp, hs, w_qkv, b_qkv, tabp, tabs, gp, gs):
    tn = TN_WIDE
    nm = gp.rows // gp.tm
    tab_specs = [pl.BlockSpec((gp.tm, LANES), lambda n, m: (m % gp.tps, 0))] * 2
    tab_specs += [pl.BlockSpec((gs.tm, LANES), lambda n, m: (0, 0))] * 2

    def call(kern, n_tiles, col_off, out_tn, out_cols, dtype, name):
        return pl.pallas_call(
            functools.partial(kern, gp=gp, gs=gs),
            grid=(n_tiles, nm),
            in_specs=_lhs_specs(gp, gs, D_MODEL) + [
                pl.BlockSpec((D_MODEL, tn), lambda n, m: (0, n + col_off)),
                pl.BlockSpec((1, tn), lambda n, m: (0, n + col_off))] + tab_specs,
            out_specs=[pl.BlockSpec((gp.tm, out_tn), lambda n, m: (m, n)),
                       pl.BlockSpec((gs.tm, out_tn), lambda n, m: (0, n))],
            out_shape=[jax.ShapeDtypeStruct((gp.rows, out_cols), dtype),
                       jax.ShapeDtypeStruct((gs.rows, out_cols), dtype)],
            scratch_shapes=[pltpu.VMEM((D_MODEL, tn), BF16)],
            compiler_params=_params(),
            name=name,
        )(hp, hs, w_qkv, b_qkv, *tabp, *tabs)

    assert KV_COLS == tn
    qp, qs = call(_q_kernel, Q_COLS // tn, 0, tn, Q_COLS, BF16, "q_rope")
    kvp, kvs = call(_kv_kernel, 2, Q_COLS // tn, KV_DUP, 2 * KV_DUP, F32, "kv_rope")
    return qp, qs, kvp, kvs


def _block_diag(x2):
    lane = lax.broadcasted_iota(jnp.int32, x2.shape, 1)
    lo = jnp.where(lane < HEAD_DIM, x2, 0.0)
    hi = jnp.where(lane >= HEAD_DIM, x2, 0.0)
    return jnp.concatenate([lo, hi], axis=0).astype(BF16)


ATTN_BLOCKS = 2


def _attn_kernel(q_ref, kvp_ref, kvc_ref, sink_ref, o_ref, *, blocks_per_seq):
    i = pl.program_id(0)
    pairs = GQA_GROUP // 2
    rows = pairs * WINDOW
    prev_bias = jnp.where((i * ATTN_BLOCKS) % blocks_per_seq > 0, 0.0, -jnp.inf)
    qi = lax.broadcasted_iota(jnp.int32, (rows, 2 * WINDOW), 0) & (WINDOW - 1)
    kj = lax.broadcasted_iota(jnp.int32, (rows, 2 * WINDOW), 1) & (WINDOW - 1)
    cur = kj <= qi
    lower = lax.broadcasted_iota(jnp.int32, (rows, LANES), 1) < HEAD_DIM
    ones_bd = _block_diag(jnp.ones((WINDOW, LANES), F32))
    nt = (((1,), (1,)), ((), ()))

    def blk(b):
        return slice(b * WINDOW, (b + 1) * WINDOW)

    def qcols(kh):
        return [slice((kh * pairs + pi) * LANES, (kh * pairs + pi + 1) * LANES) for pi in range(pairs)]

    def prev_cols(b, cols):
        return kvp_ref[:, cols] if b == 0 else kvc_ref[blk(b - 1), cols]

    def scores(b, kh):
        kc = slice(kh * LANES, (kh + 1) * LANES)
        q = jnp.concatenate([q_ref[blk(b), qc] for qc in qcols(kh)], axis=0)
        s_cur = lax.dot_general(q, _block_diag(kvc_ref[blk(b), kc]), nt, preferred_element_type=F32)
        s_prev = lax.dot_general(q, _block_diag(prev_cols(b, kc)), nt, preferred_element_type=F32)
        if b == 0:
            s_prev = s_prev + prev_bias
        return jnp.where(cur, s_cur, s_prev)

    def finish(b, kh, s):
        vc = slice(KV_DUP + kh * LANES, KV_DUP + (kh + 1) * LANES)
        sink = sink_ref[kh]
        ps, es = [], []
        for hd in range(2):
            sh = s[:, hd * WINDOW:(hd + 1) * WINDOW]
            sk = sink[:, hd * HEAD_DIM:hd * HEAD_DIM + 1]
            mx = jnp.maximum(jnp.max(sh, axis=-1, keepdims=True), sk)
            ps.append(jnp.exp(sh - mx))
            es.append(jnp.exp(sk - mx))
        p = jnp.concatenate(ps, axis=1).astype(BF16)
        zero = jnp.zeros_like(p)
        o = _dot(jnp.where(cur, p, zero), _block_diag(kvc_ref[blk(b), vc]))
        o = o + _dot(jnp.where(cur, zero, p), _block_diag(prev_cols(b, vc)))
        den = _dot(p, ones_bd) + jnp.where(lower, es[0], es[1])
        o = (o / den).astype(o_ref.dtype)
        for pi, qc in enumerate(qcols(kh)):
            o_ref[blk(b), qc] = o[pi * WINDOW:(pi + 1) * WINDOW, :]

    order = [(b, kh) for b in range(ATTN_BLOCKS) for kh in range(N_KV_HEADS)]
    pending = [scores(b, kh) for b, kh in order]
    for (b, kh), s in zip(order, pending):
        finish(b, kh, s)


def _attn_prompt(q, kv, sink_t):
    rows = q.shape[0]
    bps = SEQ // WINDOW
    assert bps % ATTN_BLOCKS == 0
    step_rows = ATTN_BLOCKS * WINDOW

    def prev_block(i):
        first = i * ATTN_BLOCKS
        return jnp.where(first % bps == 0, first, first - 1)

    return pl.pallas_call(
        functools.partial(_attn_kernel, blocks_per_seq=bps),
        grid=(rows // step_rows,),
        in_specs=[
            pl.BlockSpec((step_rows, Q_COLS), lambda i: (i, 0)),
            pl.BlockSpec((WINDOW, 2 * KV_DUP), lambda i: (prev_block(i), 0)),
            pl.BlockSpec((step_rows, 2 * KV_DUP), lambda i: (i, 0)),
            pl.BlockSpec(sink_t.shape, lambda i: (0, 0, 0)),
        ],
        out_specs=pl.BlockSpec((step_rows, Q_COLS), lambda i: (i, 0)),
        out_shape=jax.ShapeDtypeStruct((rows, Q_COLS), BF16),
        compiler_params=_params(1),
        name="attn_banded",
    )(q, kv, kv, sink_t)


def _attn_sample_kernel(q_ref, k_ref, v_ref, sink_ref, o_ref, *, n_keys, kpos0):
    rows = GQA_GROUP * DEC_SEQ
    keys = k_ref.shape[2]
    t = lax.broadcasted_iota(jnp.int32, (rows, keys), 0) % DEC_SEQ
    j = lax.broadcasted_iota(jnp.int32, (rows, keys), 1)
    mask = (j > t) & (j <= t + WINDOW) & (j < n_keys) & (j + kpos0 >= 0)
    scores = [lax.dot_general(q_ref[0, kh], k_ref[0, kh].astype(BF16), (((1,), (1,)), ((), ())),
                              preferred_element_type=F32) for kh in range(N_KV_HEADS)]
    for kh in range(N_KV_HEADS):
        s = jnp.where(mask, scores[kh], -jnp.inf)
        sk = sink_ref[kh][:, 0:1]
        mx = jnp.maximum(jnp.max(s, axis=-1, keepdims=True), sk)
        p = jnp.exp(s - mx)
        d = jnp.sum(p, axis=-1, keepdims=True) + jnp.exp(sk - mx)
        o = _dot(p.astype(BF16), v_ref[0, kh].astype(BF16))
        o_ref[0, kh] = (o / d).astype(o_ref.dtype)


def _attn_sample(q, k, v, sink_s, n_keys):
    nb, nkv, rows, d = q.shape
    keys = k.shape[2]
    return pl.pallas_call(
        functools.partial(_attn_sample_kernel, n_keys=n_keys, kpos0=PAST_LEN - WINDOW),
        grid=(nb,),
        in_specs=[
            pl.BlockSpec((1, nkv, rows, d), lambda b: (b, 0, 0, 0)),
            pl.BlockSpec((1, nkv, keys, d), lambda b: (b, 0, 0, 0)),
            pl.BlockSpec((1, nkv, keys, d), lambda b: (b, 0, 0, 0)),
            pl.BlockSpec(sink_s.shape, lambda b: (0, 0, 0)),
        ],
        out_specs=pl.BlockSpec((1, nkv, rows, d), lambda b: (b, 0, 0, 0)),
        out_shape=jax.ShapeDtypeStruct(q.shape, BF16),
        compiler_params=_params(1),
        name="attn_sample",
    )(q, k, v, sink_s)


def _ffn_up_rows(h_ref, ctxg_ref, ctxu_ref, act_ref, tailg_ref, tailu_ref, z_ref, w, cw, *, g):
    assert g.tps == 1
    tm, hdr = g.tm, g.hdr_conv
    tn = act_ref.shape[1]
    z_ref[0:hdr, 0:tn] = ctxg_ref[...]
    z_ref[0:hdr, tn:2 * tn] = ctxu_ref[...]
    for r0 in range(0, tm, g.chunk):
        up = _dot(h_ref[r0:r0 + g.chunk, :], w[...])
        lo = hdr + r0
        z_ref[lo:lo + g.chunk, :] = up
        y = _conv3(z_ref, cw, up, lo, g.shift, g.chunk)
        gate, lin = y[:, 0:tn], y[:, tn:2 * tn]
        silu = gate / (1.0 + jnp.exp(-gate))
        act_ref[r0:r0 + g.chunk, :] = (silu * lin).astype(act_ref.dtype)
    tailg_ref[...] = z_ref[tm:tm + hdr, 0:tn]
    tailu_ref[...] = z_ref[tm:tm + hdr, tn:2 * tn]


def _ffn_up_kernel(hp_ref, hs_ref, w_hbm, cwg_ref, cwu_ref,
                   ctxgp_ref, ctxup_ref, ctxgs_ref, ctxus_ref,
                   actp_ref, tailgp_ref, tailup_ref, acts_ref, tailgs_ref, tailus_ref,
                   wstage_ref, wb_ref, zp_ref, zs_ref, sem, *, gp, gs, layer, n_tiles):
    n = pl.program_id(0)
    m = pl.program_id(1)
    tn = actp_ref.shape[1]
    cw = jnp.concatenate([cwg_ref[...], cwu_ref[...]], axis=1)
    rows = functools.partial(_ffn_up_rows, w=wb_ref, cw=cw)

    def weight_copies(col_tile):
        return [pltpu.make_async_copy(
            w_hbm.at[layer, :, pl.ds(half * D_FF + col_tile * tn, tn)],
            wstage_ref.at[:, pl.ds(half * tn, tn)], sem.at[half]) for half in range(2)]

    @pl.when(jnp.logical_and(n == 0, m == 0))
    def _():
        for cp in weight_copies(0):
            cp.start()

    @pl.when(m == 0)
    def _():
        for cp in weight_copies(n):
            cp.wait()
        wb_ref[...] = wstage_ref[...].astype(BF16)
        rows(hs_ref, ctxgs_ref, ctxus_ref, acts_ref, tailgs_ref, tailus_ref, zs_ref, g=gs)

    @pl.when(jnp.logical_and(m == 1, n + 1 < n_tiles))
    def _():
        for cp in weight_copies(n + 1):
            cp.start()

    rows(hp_ref, ctxgp_ref, ctxup_ref, actp_ref, tailgp_ref, tailup_ref, zp_ref, g=gp)


def _ffn_up(hp, hs, w_up, conv_w, ctxp, ctxs, layer, gp, gs):
    tn = TN_PAIR
    nn, nm = D_FF // tn, gp.rows // gp.tm
    assert nm >= 2
    ctxgp, outp = _group_specs(gp, gp.hdr_conv, tn)
    ctxup, _ = _group_specs(gp, gp.hdr_conv, tn, nn)
    ctxgs, outs = _single_specs(gs, gs.hdr_conv, tn)
    ctxus, _ = _single_specs(gs, gs.hdr_conv, tn, nn)
    lo = lambda n, m: (layer, 0, n)
    hi = lambda n, m: (layer, 0, n + nn)
    return pl.pallas_call(
        functools.partial(_ffn_up_kernel, gp=gp, gs=gs, layer=layer, n_tiles=nn),
        grid=(nn, nm),
        in_specs=_lhs_specs(gp, gs, D_MODEL) + [
            pl.BlockSpec(memory_space=pl.ANY),
            pl.BlockSpec((None, 3, tn), lo), pl.BlockSpec((None, 3, tn), hi),
            ctxgp, ctxup, ctxgs, ctxus],
        out_specs=outp + outp[1:] + outs + outs[1:],
        out_shape=[
            jax.ShapeDtypeStruct((gp.rows, D_FF), BF16),
            jax.ShapeDtypeStruct((nm * gp.hdr_conv, D_FF), F32),
            jax.ShapeDtypeStruct((nm * gp.hdr_conv, D_FF), F32),
            jax.ShapeDtypeStruct((gs.rows, D_FF), BF16),
            jax.ShapeDtypeStruct((gs.hdr_conv, D_FF), F32),
            jax.ShapeDtypeStruct((gs.hdr_conv, D_FF), F32),
        ],
        scratch_shapes=[pltpu.VMEM((D_MODEL, 2 * tn), F32),
                        pltpu.VMEM((D_MODEL, 2 * tn), BF16),
                        pltpu.VMEM((gp.hdr_conv + gp.tm, 2 * tn), F32),
                        pltpu.VMEM((gs.hdr_conv + gs.tm, 2 * tn), F32),
                        pltpu.SemaphoreType.DMA((2,))],
        compiler_params=_params(),
        name="ffn_up",
    )(hp, hs, w_up, conv_w, conv_w, ctxp, ctxp, ctxs, ctxs)


def _norm2(xp, xs, g, gp, gs, dtype=BF16):
    return _rmsnorm(xp, g, dtype, gp.tr), _rmsnorm(xs, g, dtype, gs.tr)


def _ffn(xp, xs, gp, gp_up, gs, layer, norm_g, w_up, conv_w, w_down, ctxp, ctxs):
    hp, hs = _norm2(xp, xs, norm_g, gp, gs)
    actp, tgp, tup, acts, tgs, tus = _ffn_up(hp, hs, w_up, conv_w, ctxp, ctxs, layer, gp_up, gs)
    xs, w_down_bf16 = _ffn_down_sample(xs, acts, w_down, layer)
    xp = _ffn_down(xp, actp, w_down_bf16, tm=gp.tm_down)
    return xp, xs, jnp.concatenate([tgp, tup], axis=1), jnp.concatenate([tgs, tus], axis=1)


def _undup_heads(kv_half):
    rows = kv_half.shape[0]
    return kv_half.reshape(rows, N_KV_HEADS, 2, HEAD_DIM)[:, :, 0, :].reshape(rows, KV_COLS)


def kernel(x_prompt, x_sample, state_conv_a, state_pool, cache_win_k, cache_win_v, state_ffn_conv,
           norm_mix, w_in_ab, conv_a, w_pool, pool_scale, w_out_ab,
           w_qkv, b_qkv, sinks, w_o, norm_ffn, w_up, conv_ffn, w_down, norm_final):
    bp = x_prompt.shape[0]
    nb = x_sample.shape[0]
    gp, gs = _prompt_group(bp), _sample_group()
    gp_up = _prompt_group(bp, tm=SEQ)

    def to_tm(a):
        return jnp.swapaxes(a, 0, 1).reshape(-1, a.shape[-1])

    def from_tm(a):
        return jnp.swapaxes(a.reshape(-1, nb, a.shape[-1]), 0, 1)

    def pad_hdr(a, hdr):
        return jnp.concatenate([jnp.zeros((hdr - a.shape[0], a.shape[1]), F32), a], axis=0)

    def last_rows(tail, g, hdr, keep):
        tiles = g.rows // g.tm
        return tail.reshape(tiles, hdr, -1)[g.tps - 1:tiles:g.tps, hdr - keep:]

    def last_steps(tail, keep):
        return from_tm(tail[tail.shape[0] - keep * nb:])

    sink = sinks[0].astype(F32)
    sink_t = jnp.broadcast_to(
        sink.reshape(N_KV_HEADS, GQA_GROUP // 2, 1, 2, 1),
        (N_KV_HEADS, GQA_GROUP // 2, WINDOW, 2, HEAD_DIM)).reshape(N_KV_HEADS, -1, LANES)
    sink_s = jnp.broadcast_to(
        jnp.repeat(sink.reshape(N_KV_HEADS, GQA_GROUP), DEC_SEQ, axis=1)[:, :, None],
        (N_KV_HEADS, GQA_GROUP * DEC_SEQ, LANES))

    xp = x_prompt.reshape(gp.rows, D_MODEL)
    xs = to_tm(x_sample)
    zeros_conv = lambda c: jnp.zeros((bp * gp.hdr_conv, c), F32)

    hp, hs = _norm2(xp, xs, norm_mix[0], gp, gs)
    yap, tail_ap, yas, tail_as = _mix_a(
        hp, hs, w_in_ab[0], conv_a[0], zeros_conv(D_A), to_tm(state_conv_a[0]), gp, gs)
    ybp, tail_pp, ybs, tail_ps = _mix_b(
        hp, hs, w_in_ab[0], w_pool[0], pool_scale[0].reshape(1, D_B),
        jnp.zeros((bp * gp.hdr_pool, D_B), F32), pad_hdr(to_tm(state_pool[0]), gs.hdr_pool), gp, gs)
    xp, xs = _res_mm(xp, xs, [yap, ybp], [yas, ybs], w_out_ab[0], gp, gs, "out_proj")
    xp, xs, tail_f0p, tail_f0s = _ffn(xp, xs, gp, gp_up, gs, 0, norm_ffn[0], w_up, conv_ffn, w_down,
                                      zeros_conv(2 * D_FF), to_tm(state_ffn_conv[0]))

    hp, hs = _norm2(xp, xs, norm_mix[1], gp, gs)
    tabp = _rope_table(gp.tm * gp.tps, gp.pos0, gp.shift)
    tabs = _rope_table(gs.tm, gs.pos0, gs.shift)
    qp, qs, kvp, kvs = _qkv(hp, hs, w_qkv[0], b_qkv[0].reshape(1, -1), tabp, tabs, gp, gs)

    op = _attn_prompt(qp, kvp, sink_t)
    kv_last = kvp.reshape(bp, SEQ, 2 * KV_DUP)[:, SEQ - WINDOW:].reshape(bp * WINDOW, 2 * KV_DUP)
    k_p = _undup_heads(kv_last[:, :KV_DUP]).reshape(bp, WINDOW, N_KV_HEADS, HEAD_DIM)
    v_p = _undup_heads(kv_last[:, KV_DUP:]).reshape(bp, WINDOW, N_KV_HEADS, HEAD_DIM)

    n_keys = WINDOW + DEC_SEQ
    kk = jnp.concatenate([cache_win_k[0].reshape(nb, WINDOW, KV_COLS),
                          from_tm(_undup_heads(kvs[:, :KV_DUP]))], axis=1)
    vv = jnp.concatenate([cache_win_v[0].reshape(nb, WINDOW, KV_COLS),
                          from_tm(_undup_heads(kvs[:, KV_DUP:]))], axis=1)

    def heads_major(a):
        a = jnp.pad(a, ((0, 0), (0, 2 * WINDOW - n_keys), (0, 0)))
        return a.reshape(nb, 2 * WINDOW, N_KV_HEADS, HEAD_DIM).transpose(0, 2, 1, 3)

    q5 = from_tm(qs).reshape(nb, DEC_SEQ, N_KV_HEADS, GQA_GROUP, HEAD_DIM)
    q5 = q5.transpose(0, 2, 3, 1, 4).reshape(nb, N_KV_HEADS, GQA_GROUP * DEC_SEQ, HEAD_DIM)
    o5 = _attn_sample(q5, heads_major(kk), heads_major(vv), sink_s, n_keys)
    o5 = o5.reshape(nb, N_KV_HEADS, GQA_GROUP, DEC_SEQ, HEAD_DIM).transpose(0, 3, 1, 2, 4)
    os_ = to_tm(o5.reshape(nb, DEC_SEQ, Q_COLS))
    k_s = kk[:, n_keys - WINDOW:].reshape(nb, WINDOW, N_KV_HEADS, HEAD_DIM)
    v_s = vv[:, n_keys - WINDOW:].reshape(nb, WINDOW, N_KV_HEADS, HEAD_DIM)

    xp, xs = _res_mm(xp, xs, [op], [os_], w_o[0], gp, gs, "attn_out")
    xp, xs, tail_f1p, tail_f1s = _ffn(xp, xs, gp, gp_up, gs, 1, norm_ffn[1], w_up, conv_ffn, w_down,
                                      zeros_conv(2 * D_FF), to_tm(state_ffn_conv[1]))
    yp, ys = _norm2(xp, xs, norm_final, gp, gs, F32)

    ca_p = last_rows(tail_ap, gp, gp.hdr_conv, 2)[None]
    pl_p = last_rows(tail_pp, gp, gp.hdr_pool, POOL_CTX)[None]
    f_p = jnp.stack([last_rows(t, gp_up, gp_up.hdr_conv, 2) for t in (tail_f0p, tail_f1p)])
    ca_s = last_steps(tail_as, 2)[None]
    pl_s = last_steps(tail_ps, POOL_CTX)[None]
    f_s = jnp.stack([last_steps(t, 2) for t in (tail_f0s, tail_f1s)])
    return (yp.reshape(bp, SEQ, D_MODEL), from_tm(ys), ca_p, ca_s, pl_p, pl_s,
            k_p[None], k_s[None], v_p[None], v_s[None], f_p, f_s)
```

```python
import functools
from typing import NamedTuple

import jax
import jax.numpy as jnp
from jax import lax
from jax.experimental import pallas as pl
from jax.experimental.pallas import tpu as pltpu

D_MODEL = 4096
SEQ = 2048
DEC_BATCH = 32
DEC_SEQ = 4
PAST_LEN = 16384
D_A = D_MODEL // 2
D_B = D_MODEL // 2
POOL_WINDOWS = (2, 4, 8, 16)
POOL_GROUP = D_B // len(POOL_WINDOWS)
POOL_CTX = max(POOL_WINDOWS) - 1
HEAD_DIM = 64
N_HEADS = D_MODEL // HEAD_DIM
N_KV_HEADS = 8
GQA_GROUP = N_HEADS // N_KV_HEADS
WINDOW = 128
ROPE_THETA = 10000.0
D_FF = 11008
EPS = 1e-6

F32 = jnp.float32
BF16 = jnp.bfloat16

LANES = 128
SUBLANES = 8
VMEM_LIMIT = 56 * 1024 * 1024
Q_COLS = N_HEADS * HEAD_DIM
KV_COLS = N_KV_HEADS * HEAD_DIM
KV_DUP = N_KV_HEADS * LANES
TN_WIDE = 512
TN_PAIR = 256


class _Group(NamedTuple):
    rows: int
    tm: int
    tps: int
    shift: int
    pos0: int
    chunk: int
    hdr_conv: int
    hdr_pool: int
    tm_down: int
    tr: int


def _prompt_group(batch, tm=1024):
    return _Group(rows=batch * SEQ, tm=tm, tps=SEQ // tm, shift=1, pos0=0, chunk=256,
                  hdr_conv=SUBLANES, hdr_pool=POOL_CTX + 1, tm_down=512, tr=512)


def _sample_group():
    rows = DEC_BATCH * DEC_SEQ
    return _Group(rows=rows, tm=rows, tps=1, shift=DEC_BATCH, pos0=PAST_LEN, chunk=rows,
                  hdr_conv=2 * DEC_BATCH, hdr_pool=(POOL_CTX + 1) * DEC_BATCH, tm_down=rows, tr=rows)


def _params(n_axes=2):
    return pltpu.CompilerParams(
        dimension_semantics=("arbitrary",) * n_axes, vmem_limit_bytes=VMEM_LIMIT)


def _dot(a, b):
    return jnp.dot(a, b, preferred_element_type=F32)


def _div_pow2(x, d):
    assert d & (d - 1) == 0
    return lax.shift_right_logical(x, d.bit_length() - 1)


def _round_weights(pairs):
    for src, dst in pairs:
        dst[...] = src[...].astype(BF16)


def _norm_kernel(x_ref, g_ref, o_ref):
    x = x_ref[...]
    y = x * lax.rsqrt(jnp.mean(x * x, axis=-1, keepdims=True) + EPS)
    o_ref[...] = (y * g_ref[...]).astype(o_ref.dtype)


def _rmsnorm(x, g, out_dtype, tr):
    rows, d = x.shape
    return pl.pallas_call(
        _norm_kernel,
        grid=(rows // tr,),
        in_specs=[pl.BlockSpec((tr, d), lambda i: (i, 0)),
                  pl.BlockSpec((1, d), lambda i: (0, 0))],
        out_specs=pl.BlockSpec((tr, d), lambda i: (i, 0)),
        out_shape=jax.ShapeDtypeStruct((rows, d), out_dtype),
        compiler_params=_params(1),
        name="rmsnorm",
    )(x, g.reshape(1, d))


def _set_header(z_ref, ctx_ref, first, tm, hdr):
    if first is None:
        z_ref[0:hdr, :] = ctx_ref[...]
        return

    @pl.when(first)
    def _():
        z_ref[0:hdr, :] = ctx_ref[...]

    @pl.when(jnp.logical_not(first))
    def _():
        z_ref[0:hdr, :] = z_ref[tm:tm + hdr, :]


def _conv3(z_ref, cw, cur, lo, shift, rows):
    y = z_ref[lo - 2 * shift:lo - 2 * shift + rows, :] * cw[0:1, :]
    y = y + z_ref[lo - shift:lo - shift + rows, :] * cw[1:2, :]
    return y + cur * cw[2:3, :]


def _group_specs(g, hdr, tn, col_off=0):
    ctx = pl.BlockSpec((hdr, tn), lambda n, m: (m // g.tps, n + col_off))
    outs = [pl.BlockSpec((g.tm, tn), lambda n, m: (m, n)),
            pl.BlockSpec((hdr, tn), lambda n, m: (m, n))]
    return ctx, outs


def _single_specs(g, hdr, tn, col_off=0):
    ctx = pl.BlockSpec((hdr, tn), lambda n, m: (0, n + col_off))
    outs = [pl.BlockSpec((g.tm, tn), lambda n, m: (0, n)),
            pl.BlockSpec((hdr, tn), lambda n, m: (0, n))]
    return ctx, outs


def _lhs_specs(gp, gs, k):
    return [pl.BlockSpec((gp.tm, k), lambda n, m: (m, 0)),
            pl.BlockSpec((gs.tm, k), lambda n, m: (0, 0))]


def _mix_a_rows(h_ref, ctx_ref, y_ref, tail_ref, z_ref, wb, wc, wx, cw, *, g, first):
    tm, hdr = g.tm, g.hdr_conv
    _set_header(z_ref, ctx_ref, first, tm, hdr)
    for r0 in range(0, tm, g.chunk):
        hc = h_ref[r0:r0 + g.chunk, :]
        gate_b = _dot(hc, wb[...])
        u = _dot(hc, wc[...]) * _dot(hc, wx[...])
        z_ref[hdr + r0:hdr + r0 + g.chunk, :] = u
        conv = _conv3(z_ref, cw, u, hdr + r0, g.shift, g.chunk)
        y_ref[r0:r0 + g.chunk, :] = (gate_b * conv).astype(y_ref.dtype)
    tail_ref[...] = z_ref[tm:tm + hdr, :]


def _mix_a_kernel(hp_ref, hs_ref, wb_ref, wc_ref, wx_ref, cw_ref, ctxp_ref, ctxs_ref,
                  yp_ref, tailp_ref, ys_ref, tails_ref,
                  wbb_ref, wcb_ref, wxb_ref, zp_ref, zs_ref, *, gp, gs):
    m = pl.program_id(1)
    cw = cw_ref[...]
    rows = functools.partial(_mix_a_rows, wb=wbb_ref, wc=wcb_ref, wx=wxb_ref, cw=cw)

    @pl.when(m == 0)
    def _():
        _round_weights([(wb_ref, wbb_ref), (wc_ref, wcb_ref), (wx_ref, wxb_ref)])
        rows(hs_ref, ctxs_ref, ys_ref, tails_ref, zs_ref, g=gs, first=None)

    rows(hp_ref, ctxp_ref, yp_ref, tailp_ref, zp_ref, g=gp, first=m % gp.tps == 0)


def _mix_a(hp, hs, w_in, conv_w, ctxp, ctxs, gp, gs):
    tn = TN_PAIR
    nn, nm = D_A // tn, gp.rows // gp.tm
    ctxp_spec, outp = _group_specs(gp, gp.hdr_conv, tn)
    ctxs_spec, outs = _single_specs(gs, gs.hdr_conv, tn)
    wspec = lambda off: pl.BlockSpec((D_MODEL, tn), lambda n, m: (0, n + off))
    return pl.pallas_call(
        functools.partial(_mix_a_kernel, gp=gp, gs=gs),
        grid=(nn, nm),
        in_specs=_lhs_specs(gp, gs, D_MODEL) + [
            wspec(0), wspec(nn), wspec(2 * nn),
            pl.BlockSpec((3, tn), lambda n, m: (0, n)),
            ctxp_spec, ctxs_spec],
        out_specs=outp + outs,
        out_shape=[
            jax.ShapeDtypeStruct((gp.rows, D_A), BF16),
            jax.ShapeDtypeStruct((nm * gp.hdr_conv, D_A), F32),
            jax.ShapeDtypeStruct((gs.rows, D_A), BF16),
            jax.ShapeDtypeStruct((gs.hdr_conv, D_A), F32),
        ],
        scratch_shapes=[pltpu.VMEM((D_MODEL, tn), BF16)] * 3 + [
            pltpu.VMEM((gp.hdr_conv + gp.tm, tn), F32), pltpu.VMEM((gs.hdr_conv + gs.tm, tn), F32)],
        compiler_params=_params(),
        name="mix_a",
    )(hp, hs, w_in, w_in, w_in, conv_w, ctxp, ctxs)


def _mix_b_rows(h_ref, ctx_ref, y_ref, tail_ref, z_ref, wp, wg, scale, *, g, first, tile, win):
    tm, hdr = g.tm, g.hdr_pool
    _set_header(z_ref, ctx_ref, first, tm, hdr)
    for r0 in range(0, tm, g.chunk):
        z_ref[hdr + r0:hdr + r0 + g.chunk, :] = _dot(h_ref[r0:r0 + g.chunk, :], wp[...])
    for r0 in range(0, tm, g.chunk):
        lo = hdr + r0
        p = z_ref[lo:lo + g.chunk, :]
        acc = p
        for i in range(1, win):
            acc = acc + z_ref[lo - i * g.shift:lo - i * g.shift + g.chunk, :]
        row = lax.broadcasted_iota(jnp.int32, (g.chunk, 1), 0) + (tile * tm + r0)
        pos = g.pos0 + _div_pow2(row, g.shift)
        cnt = jnp.minimum(pos + 1, win).astype(F32)
        pooled = acc / cnt - p
        yb = _dot(pooled.astype(BF16), wg[...]) * scale
        y_ref[r0:r0 + g.chunk, :] = yb.astype(y_ref.dtype)
    tail_ref[...] = z_ref[tm:tm + hdr, :]


def _mix_b_kernel(hp_ref, hs_ref, wp_ref, wg_ref, sc_ref, ctxp_ref, ctxs_ref,
                  yp_ref, tailp_ref, ys_ref, tails_ref,
                  wpb_ref, wgb_ref, zp_ref, zs_ref, *, gp, gs):
    grp = pl.program_id(0)
    m = pl.program_id(1)
    scale = sc_ref[...]

    @pl.when(m == 0)
    def _():
        _round_weights([(wp_ref, wpb_ref), (wg_ref, wgb_ref)])

    for gi, win in enumerate(POOL_WINDOWS):
        rows = functools.partial(_mix_b_rows, wp=wpb_ref, wg=wgb_ref, scale=scale, win=win)

        @pl.when(jnp.logical_and(grp == gi, m == 0))
        def _(rows=rows):
            rows(hs_ref, ctxs_ref, ys_ref, tails_ref, zs_ref, g=gs, first=None, tile=0)

        @pl.when(grp == gi)
        def _(rows=rows):
            rows(hp_ref, ctxp_ref, yp_ref, tailp_ref, zp_ref, g=gp,
                 first=m % gp.tps == 0, tile=m % gp.tps)


def _mix_b(hp, hs, w_in, w_grp, scale, ctxp, ctxs, gp, gs):
    tn, ng = POOL_GROUP, len(POOL_WINDOWS)
    nm = gp.rows // gp.tm
    col0 = 3 * D_A // tn
    ctxp_spec, outp = _group_specs(gp, gp.hdr_pool, tn)
    ctxs_spec, outs = _single_specs(gs, gs.hdr_pool, tn)
    return pl.pallas_call(
        functools.partial(_mix_b_kernel, gp=gp, gs=gs),
        grid=(ng, nm),
        in_specs=_lhs_specs(gp, gs, D_MODEL) + [
            pl.BlockSpec((D_MODEL, tn), lambda n, m: (0, col0 + n)),
            pl.BlockSpec((None, tn, tn), lambda n, m: (n, 0, 0)),
            pl.BlockSpec((1, tn), lambda n, m: (0, n)),
            ctxp_spec, ctxs_spec],
        out_specs=outp + outs,
        out_shape=[
            jax.ShapeDtypeStruct((gp.rows, D_B), BF16),
            jax.ShapeDtypeStruct((nm * gp.hdr_pool, D_B), F32),
            jax.ShapeDtypeStruct((gs.rows, D_B), BF16),
            jax.ShapeDtypeStruct((gs.hdr_pool, D_B), F32),
        ],
        scratch_shapes=[
            pltpu.VMEM((D_MODEL, tn), BF16), pltpu.VMEM((tn, tn), BF16),
            pltpu.VMEM((gp.hdr_pool + gp.tm, tn), F32), pltpu.VMEM((gs.hdr_pool + gs.tm, tn), F32)],
        compiler_params=_params(),
        name="mix_b",
    )(hp, hs, w_in, w_grp, scale, ctxp, ctxs)


def _res_mm_kernel(*refs, n_lhs):
    m = pl.program_id(1)
    xp_ref, xs_ref = refs[0], refs[1]
    ap = refs[2:2 + n_lhs]
    a_s = refs[2 + n_lhs:2 + 2 * n_lhs]
    w = refs[2 + 2 * n_lhs:2 + 3 * n_lhs]
    op_ref, os_ref = refs[2 + 3 * n_lhs], refs[3 + 3 * n_lhs]
    wb = refs[4 + 3 * n_lhs:]

    def rows(x_ref, a_refs, o_ref):
        acc = x_ref[...]
        for a_ref, wb_ref in zip(a_refs, wb):
            acc = acc + _dot(a_ref[...], wb_ref[...])
        o_ref[...] = acc

    @pl.when(m == 0)
    def _():
        _round_weights(list(zip(w, wb)))
        rows(xs_ref, a_s, os_ref)

    rows(xp_ref, ap, op_ref)


def _res_mm(xp, xs, lhs_p, lhs_s, w, gp, gs, name):
    d_out = xp.shape[1]
    n_lhs = len(lhs_p)
    kdim = lhs_p[0].shape[1]
    tn = TN_WIDE
    xspec = [pl.BlockSpec((gp.tm, tn), lambda n, m: (m, n)),
             pl.BlockSpec((gs.tm, tn), lambda n, m: (0, n))]
    lhs_specs = _lhs_specs(gp, gs, kdim)
    in_specs = xspec + [lhs_specs[0]] * n_lhs + [lhs_specs[1]] * n_lhs
    in_specs += [pl.BlockSpec((kdim, tn), lambda n, m, i=i: (i, n)) for i in range(n_lhs)]
    return pl.pallas_call(
        functools.partial(_res_mm_kernel, n_lhs=n_lhs),
        grid=(d_out // tn, gp.rows // gp.tm),
        in_specs=in_specs,
        out_specs=xspec,
        out_shape=[jax.ShapeDtypeStruct(xp.shape, F32), jax.ShapeDtypeStruct(xs.shape, F32)],
        scratch_shapes=[pltpu.VMEM((kdim, tn), BF16)] * n_lhs,
        compiler_params=_params(),
        name=name,
    )(xp, xs, *lhs_p, *lhs_s, *([w] * n_lhs))


def _down_kernel(x_ref, a_ref, w_ref, o_ref):
    o_ref[...] = x_ref[...] + _dot(a_ref[...], w_ref[...])


def _ffn_down(x, act, w_bf16, *, tm):
    rows, d_out = x.shape
    kdim = act.shape[1]
    tn = TN_WIDE
    return pl.pallas_call(
        _down_kernel,
        grid=(rows // tm, d_out // tn),
        in_specs=[pl.BlockSpec((tm, tn), lambda m, n: (m, n)),
                  pl.BlockSpec((tm, kdim), lambda m, n: (m, 0)),
                  pl.BlockSpec((kdim, tn), lambda m, n: (0, n))],
        out_specs=pl.BlockSpec((tm, tn), lambda m, n: (m, n)),
        out_shape=jax.ShapeDtypeStruct((rows, d_out), F32),
        compiler_params=_params(),
        name="ffn_down",
    )(x, act, w_bf16)


def _down_round_kernel(x_ref, a_ref, w_ref, o_ref, wb_ref):
    wb = w_ref[...].astype(BF16)
    wb_ref[...] = wb
    o_ref[...] = x_ref[...] + _dot(a_ref[...], wb)


def _ffn_down_sample(x, act, w, layer):
    rows, d_out = x.shape
    kdim = act.shape[1]
    tn = TN_PAIR
    return pl.pallas_call(
        _down_round_kernel,
        grid=(d_out // tn,),
        in_specs=[pl.BlockSpec((rows, tn), lambda n: (0, n)),
                  pl.BlockSpec((rows, kdim), lambda n: (0, 0)),
                  pl.BlockSpec((None, kdim, tn), lambda n: (layer, 0, n))],
        out_specs=[pl.BlockSpec((rows, tn), lambda n: (0, n)),
                   pl.BlockSpec((kdim, tn), lambda n: (0, n))],
        out_shape=[jax.ShapeDtypeStruct((rows, d_out), F32),
                   jax.ShapeDtypeStruct((kdim, d_out), BF16)],
        compiler_params=_params(1),
        name="ffn_down_sample",
    )(x, act, w)


def _rope_table_kernel(inv_ref, cos_ref, sin_ref, *, pos0, shift):
    rows = cos_ref.shape[0]
    row = lax.broadcasted_iota(jnp.int32, (rows, LANES), 0)
    lane = lax.broadcasted_iota(jnp.int32, (rows, LANES), 1)
    pos = (pos0 + _div_pow2(row, shift)).astype(F32)
    ang = pos * inv_ref[...]
    first_half = (lane & (HEAD_DIM - 1)) < (HEAD_DIM // 2)
    cos_ref[...] = jnp.cos(ang)
    sin = jnp.sin(ang)
    sin_ref[...] = jnp.where(first_half, -sin, sin)


def _rope_table(rows, pos0, shift):
    half = HEAD_DIM // 2
    inv = ROPE_THETA ** (-jnp.arange(half, dtype=F32) / half)
    inv = jnp.tile(inv, LANES // half).reshape(1, LANES)
    return pl.pallas_call(
        functools.partial(_rope_table_kernel, pos0=pos0, shift=shift),
        out_shape=[jax.ShapeDtypeStruct((rows, LANES), F32)] * 2,
        name="rope_table",
    )(inv)


def _rope(x, cos_ref, sin_ref, r0):
    rows, tn = x.shape
    reps = tn // LANES
    cos = jnp.concatenate([cos_ref[r0:r0 + rows, :]] * reps, axis=1)
    sin = jnp.concatenate([sin_ref[r0:r0 + rows, :]] * reps, axis=1)
    lane = lax.broadcasted_iota(jnp.int32, x.shape, 1)
    half = HEAD_DIM // 2
    first_half = (lane & (HEAD_DIM - 1)) < half
    partner = jnp.where(first_half, pltpu.roll(x, tn - half, 1), pltpu.roll(x, half, 1))
    return x * cos + partner * sin


def _dup_heads(x):
    lane = lax.broadcasted_iota(jnp.int32, (x.shape[0], LANES), 1)
    lower = lane < HEAD_DIM
    out = []
    for c in range(x.shape[1] // LANES):
        v = x[:, c * LANES:(c + 1) * LANES]
        r = pltpu.roll(v, HEAD_DIM, 1)
        out += [jnp.where(lower, v, r), jnp.where(lower, r, v)]
    return jnp.concatenate(out, axis=1)


def _q_rows(h_ref, cos_ref, sin_ref, q_ref, wq, bias, *, g):
    for r0 in range(0, g.tm, g.chunk):
        acc = _dot(h_ref[r0:r0 + g.chunk, :], wq[...]) + bias
        roped = _rope(acc, cos_ref, sin_ref, r0)
        q_ref[r0:r0 + g.chunk, :] = (roped * (HEAD_DIM ** -0.5)).astype(q_ref.dtype)


def _q_kernel(hp_ref, hs_ref, w_ref, b_ref, cosp_ref, sinp_ref, coss_ref, sins_ref,
              qp_ref, qs_ref, wb_ref, *, gp, gs):
    m = pl.program_id(1)
    bias = b_ref[...]

    @pl.when(m == 0)
    def _():
        _round_weights([(w_ref, wb_ref)])
        _q_rows(hs_ref, coss_ref, sins_ref, qs_ref, wb_ref, bias, g=gs)

    _q_rows(hp_ref, cosp_ref, sinp_ref, qp_ref, wb_ref, bias, g=gp)


def _kv_rows(h_ref, cos_ref, sin_ref, kv_ref, wkv, bias, is_key, *, g):
    for r0 in range(0, g.tm, g.chunk):
        acc = _dot(h_ref[r0:r0 + g.chunk, :], wkv[...]) + bias
        roped = _rope(acc, cos_ref, sin_ref, r0)
        kv_ref[r0:r0 + g.chunk, :] = _dup_heads(jnp.where(is_key, roped, acc))


def _kv_kernel(hp_ref, hs_ref, w_ref, b_ref, cosp_ref, sinp_ref, coss_ref, sins_ref,
               kvp_ref, kvs_ref, wb_ref, *, gp, gs):
    is_key = pl.program_id(0) == 0
    m = pl.program_id(1)
    bias = b_ref[...]

    @pl.when(m == 0)
    def _():
        _round_weights([(w_ref, wb_ref)])
        _kv_rows(hs_ref, coss_ref, sins_ref, kvs_ref, wb_ref, bias, is_key, g=gs)

    _kv_rows(hp_ref, cosp_ref, sinp_ref, kvp_ref, wb_ref, bias, is_key, g=gp)


def _qkv(hp, hs, w_qkv, b_qkv, tabp, tabs, gp, gs):
    tn = TN_WIDE
    nm = gp.rows // gp.tm
    tab_specs = [pl.BlockSpec((gp.tm, LANES), lambda n, m: (m % gp.tps, 0))] * 2
    tab_specs += [pl.BlockSpec((gs.tm, LANES), lambda n, m: (0, 0))] * 2

    def call(kern, n_tiles, col_off, out_tn, out_cols, dtype, name):
        return pl.pallas_call(
            functools.partial(kern, gp=gp, gs=gs),
            grid=(n_tiles, nm),
            in_specs=_lhs_specs(gp, gs, D_MODEL) + [
                pl.BlockSpec((D_MODEL, tn), lambda n, m: (0, n + col_off)),
                pl.BlockSpec((1, tn), lambda n, m: (0, n + col_off))] + tab_specs,
            out_specs=[pl.BlockSpec((gp.tm, out_tn), lambda n, m: (m, n)),
                       pl.BlockSpec((gs.tm, out_tn), lambda n, m: (0, n))],
            out_shape=[jax.ShapeDtypeStruct((gp.rows, out_cols), dtype),
                       jax.ShapeDtypeStruct((gs.rows, out_cols), dtype)],
            scratch_shapes=[pltpu.VMEM((D_MODEL, tn), BF16)],
            compiler_params=_params(),
            name=name,
        )(hp, hs, w_qkv, b_qkv, *tabp, *tabs)

    assert KV_COLS == tn
    qp, qs = call(_q_kernel, Q_COLS // tn, 0, tn, Q_COLS, BF16, "q_rope")
    kvp, kvs = call(_kv_kernel, 2, Q_COLS // tn, KV_DUP, 2 * KV_DUP, F32, "kv_rope")
    return qp, qs, kvp, kvs


def _block_diag(x2):
    lane = lax.broadcasted_iota(jnp.int32, x2.shape, 1)
    lo = jnp.where(lane < HEAD_DIM, x2, 0.0)
    hi = jnp.where(lane >= HEAD_DIM, x2, 0.0)
    return jnp.concatenate([lo, hi], axis=0).astype(BF16)


ATTN_BLOCKS = 2


def _attn_kernel(q_ref, kvp_ref, kvc_ref, sink_ref, o_ref, *, blocks_per_seq):
    i = pl.program_id(0)
    pairs = GQA_GROUP // 2
    rows = pairs * WINDOW
    prev_bias = jnp.where((i * ATTN_BLOCKS) % blocks_per_seq > 0, 0.0, -jnp.inf)
    qi = lax.broadcasted_iota(jnp.int32, (rows, 2 * WINDOW), 0) & (WINDOW - 1)
    kj = lax.broadcasted_iota(jnp.int32, (rows, 2 * WINDOW), 1) & (WINDOW - 1)
    cur = kj <= qi
    lower = lax.broadcasted_iota(jnp.int32, (rows, LANES), 1) < HEAD_DIM
    ones_bd = _block_diag(jnp.ones((WINDOW, LANES), F32))
    nt = (((1,), (1,)), ((), ()))

    def blk(b):
        return slice(b * WINDOW, (b + 1) * WINDOW)

    def qcols(kh):
        return [slice((kh * pairs + pi) * LANES, (kh * pairs + pi + 1) * LANES) for pi in range(pairs)]

    def prev_cols(b, cols):
        return kvp_ref[:, cols] if b == 0 else kvc_ref[blk(b - 1), cols]

    def scores(b, kh):
        kc = slice(kh * LANES, (kh + 1) * LANES)
        q = jnp.concatenate([q_ref[blk(b), qc] for qc in qcols(kh)], axis=0)
        s_cur = lax.dot_general(q, _block_diag(kvc_ref[blk(b), kc]), nt, preferred_element_type=F32)
        s_prev = lax.dot_general(q, _block_diag(prev_cols(b, kc)), nt, preferred_element_type=F32)
        if b == 0:
            s_prev = s_prev + prev_bias
        return jnp.where(cur, s_cur, s_prev)

    def finish(b, kh, s):
        vc = slice(KV_DUP + kh * LANES, KV_DUP + (kh + 1) * LANES)
        sink = sink_ref[kh]
        ps, es = [], []
        for hd in range(2):
            sh = s[:, hd * WINDOW:(hd + 1) * WINDOW]
            sk = sink[:, hd * HEAD_DIM:hd * HEAD_DIM + 1]
            mx = jnp.maximum(jnp.max(sh, axis=-1, keepdims=True), sk)
            ps.append(jnp.exp(sh - mx))
            es.append(jnp.exp(sk - mx))
        p = jnp.concatenate(ps, axis=1).astype(BF16)
        zero = jnp.zeros_like(p)
        o = _dot(jnp.where(cur, p, zero), _block_diag(kvc_ref[blk(b), vc]))
        o = o + _dot(jnp.where(cur, zero, p), _block_diag(prev_cols(b, vc)))
        den = _dot(p, ones_bd) + jnp.where(lower, es[0], es[1])
        o = (o / den).astype(o_ref.dtype)
        for pi, qc in enumerate(qcols(kh)):
            o_ref[blk(b), qc] = o[pi * WINDOW:(pi + 1) * WINDOW, :]

    order = [(b, kh) for b in range(ATTN_BLOCKS) for kh in range(N_KV_HEADS)]
    pending = [scores(b, kh) for b, kh in order]
    for (b, kh), s in zip(order, pending):
        finish(b, kh, s)


def _attn_prompt(q, kv, sink_t):
    rows = q.shape[0]
    bps = SEQ // WINDOW
    assert bps % ATTN_BLOCKS == 0
    step_rows = ATTN_BLOCKS * WINDOW

    def prev_block(i):
        first = i * ATTN_BLOCKS
        return jnp.where(first % bps == 0, first, first - 1)

    return pl.pallas_call(
        functools.partial(_attn_kernel, blocks_per_seq=bps),
        grid=(rows // step_rows,),
        in_specs=[
            pl.BlockSpec((step_rows, Q_COLS), lambda i: (i, 0)),
            pl.BlockSpec((WINDOW, 2 * KV_DUP), lambda i: (prev_block(i), 0)),
            pl.BlockSpec((step_rows, 2 * KV_DUP), lambda i: (i, 0)),
            pl.BlockSpec(sink_t.shape, lambda i: (0, 0, 0)),
        ],
        out_specs=pl.BlockSpec((step_rows, Q_COLS), lambda i: (i, 0)),
        out_shape=jax.ShapeDtypeStruct((rows, Q_COLS), BF16),
        compiler_params=_params(1),
        name="attn_banded",
    )(q, kv, kv, sink_t)


def _attn_sample_kernel(q_ref, k_ref, v_ref, sink_ref, o_ref, *, n_keys, kpos0):
    rows = GQA_GROUP * DEC_SEQ
    keys = k_ref.shape[2]
    t = lax.broadcasted_iota(jnp.int32, (rows, keys), 0) % DEC_SEQ
    j = lax.broadcasted_iota(jnp.int32, (rows, keys), 1)
    mask = (j > t) & (j <= t + WINDOW) & (j < n_keys) & (j + kpos0 >= 0)
    scores = [lax.dot_general(q_ref[0, kh], k_ref[0, kh].astype(BF16), (((1,), (1,)), ((), ())),
                              preferred_element_type=F32) for kh in range(N_KV_HEADS)]
    for kh in range(N_KV_HEADS):
        s = jnp.where(mask, scores[kh], -jnp.inf)
        sk = sink_ref[kh][:, 0:1]
        mx = jnp.maximum(jnp.max(s, axis=-1, keepdims=True), sk)
        p = jnp.exp(s - mx)
        d = jnp.sum(p, axis=-1, keepdims=True) + jnp.exp(sk - mx)
        o = _dot(p.astype(BF16), v_ref[0, kh].astype(BF16))
        o_ref[0, kh] = (o / d).astype(o_ref.dtype)


def _attn_sample(q, k, v, sink_s, n_keys):
    nb, nkv, rows, d = q.shape
    keys = k.shape[2]
    return pl.pallas_call(
        functools.partial(_attn_sample_kernel, n_keys=n_keys, kpos0=PAST_LEN - WINDOW),
        grid=(nb,),
        in_specs=[
            pl.BlockSpec((1, nkv, rows, d), lambda b: (b, 0, 0, 0)),
            pl.BlockSpec((1, nkv, keys, d), lambda b: (b, 0, 0, 0)),
            pl.BlockSpec((1, nkv, keys, d), lambda b: (b, 0, 0, 0)),
            pl.BlockSpec(sink_s.shape, lambda b: (0, 0, 0)),
        ],
        out_specs=pl.BlockSpec((1, nkv, rows, d), lambda b: (b, 0, 0, 0)),
        out_shape=jax.ShapeDtypeStruct(q.shape, BF16),
        compiler_params=_params(1),
        name="attn_sample",
    )(q, k, v, sink_s)


def _ffn_up_rows(h_ref, ctxg_ref, ctxu_ref, act_ref, tailg_ref, tailu_ref, z_ref, w, cw, *, g):
    assert g.tps == 1
    tm, hdr = g.tm, g.hdr_conv
    tn = act_ref.shape[1]
    z_ref[0:hdr, 0:tn] = ctxg_ref[...]
    z_ref[0:hdr, tn:2 * tn] = ctxu_ref[...]
    for r0 in range(0, tm, g.chunk):
        up = _dot(h_ref[r0:r0 + g.chunk, :], w[...])
        lo = hdr + r0
        z_ref[lo:lo + g.chunk, :] = up
        y = _conv3(z_ref, cw, up, lo, g.shift, g.chunk)
        gate, lin = y[:, 0:tn], y[:, tn:2 * tn]
        half_gate = 0.5 * gate
        silu = half_gate * jnp.tanh(half_gate) + half_gate
        act_ref[r0:r0 + g.chunk, :] = (silu * lin).astype(act_ref.dtype)
    tailg_ref[...] = z_ref[tm:tm + hdr, 0:tn]
    tailu_ref[...] = z_ref[tm:tm + hdr, tn:2 * tn]


def _ffn_up_kernel(hp_ref, hs_ref, w_hbm, cwg_ref, cwu_ref,
                   ctxgp_ref, ctxup_ref, ctxgs_ref, ctxus_ref,
                   actp_ref, tailgp_ref, tailup_ref, acts_ref, tailgs_ref, tailus_ref,
                   wstage_ref, wb_ref, zp_ref, zs_ref, sem, *, gp, gs, layer, n_tiles):
    n = pl.program_id(0)
    m = pl.program_id(1)
    tn = actp_ref.shape[1]
    cw = jnp.concatenate([cwg_ref[...], cwu_ref[...]], axis=1)
    rows = functools.partial(_ffn_up_rows, w=wb_ref, cw=cw)

    def weight_copies(col_tile):
        return [pltpu.make_async_copy(
            w_hbm.at[layer, :, pl.ds(half * D_FF + col_tile * tn, tn)],
            wstage_ref.at[:, pl.ds(half * tn, tn)], sem.at[half]) for half in range(2)]

    @pl.when(jnp.logical_and(n == 0, m == 0))
    def _():
        for cp in weight_copies(0):
            cp.start()

    @pl.when(m == 0)
    def _():
        for cp in weight_copies(n):
            cp.wait()
        wb_ref[...] = wstage_ref[...].astype(BF16)
        rows(hs_ref, ctxgs_ref, ctxus_ref, acts_ref, tailgs_ref, tailus_ref, zs_ref, g=gs)

    @pl.when(jnp.logical_and(m == 1, n + 1 < n_tiles))
    def _():
        for cp in weight_copies(n + 1):
            cp.start()

    rows(hp_ref, ctxgp_ref, ctxup_ref, actp_ref, tailgp_ref, tailup_ref, zp_ref, g=gp)


def _ffn_up(hp, hs, w_up, conv_w, ctxp, ctxs, layer, gp, gs):
    tn = TN_PAIR
    nn, nm = D_FF // tn, gp.rows // gp.tm
    assert nm >= 2
    ctxgp, outp = _group_specs(gp, gp.hdr_conv, tn)
    ctxup, _ = _group_specs(gp, gp.hdr_conv, tn, nn)
    ctxgs, outs = _single_specs(gs, gs.hdr_conv, tn)
    ctxus, _ = _single_specs(gs, gs.hdr_conv, tn, nn)
    lo = lambda n, m: (layer, 0, n)
    hi = lambda n, m: (layer, 0, n + nn)
    return pl.pallas_call(
        functools.partial(_ffn_up_kernel, gp=gp, gs=gs, layer=layer, n_tiles=nn),
        grid=(nn, nm),
        in_specs=_lhs_specs(gp, gs, D_MODEL) + [
            pl.BlockSpec(memory_space=pl.ANY),
            pl.BlockSpec((None, 3, tn), lo), pl.BlockSpec((None, 3, tn), hi),
            ctxgp, ctxup, ctxgs, ctxus],
        out_specs=outp + outp[1:] + outs + outs[1:],
        out_shape=[
            jax.ShapeDtypeStruct((gp.rows, D_FF), BF16),
            jax.ShapeDtypeStruct((nm * gp.hdr_conv, D_FF), F32),
            jax.ShapeDtypeStruct((nm * gp.hdr_conv, D_FF), F32),
            jax.ShapeDtypeStruct((gs.rows, D_FF), BF16),
            jax.ShapeDtypeStruct((gs.hdr_conv, D_FF), F32),
            jax.ShapeDtypeStruct((gs.hdr_conv, D_FF), F32),
        ],
        scratch_shapes=[pltpu.VMEM((D_MODEL, 2 * tn), F32),
                        pltpu.VMEM((D_MODEL, 2 * tn), BF16),
                        pltpu.VMEM((gp.hdr_conv + gp.tm, 2 * tn), F32),
                        pltpu.VMEM((gs.hdr_conv + gs.tm, 2 * tn), F32),
                        pltpu.SemaphoreType.DMA((2,))],
        compiler_params=_params(),
        name="ffn_up",
    )(hp, hs, w_up, conv_w, conv_w, ctxp, ctxp, ctxs, ctxs)


def _norm2(xp, xs, g, gp, gs, dtype=BF16):
    return _rmsnorm(xp, g, dtype, gp.tr), _rmsnorm(xs, g, dtype, gs.tr)


def _ffn(xp, xs, gp, gp_up, gs, layer, norm_g, w_up, conv_w, w_down, ctxp, ctxs):
    hp, hs = _norm2(xp, xs, norm_g, gp, gs)
    actp, tgp, tup, acts, tgs, tus = _ffn_up(hp, hs, w_up, conv_w, ctxp, ctxs, layer, gp_up, gs)
    xs, w_down_bf16 = _ffn_down_sample(xs, acts, w_down, layer)
    xp = _ffn_down(xp, actp, w_down_bf16, tm=gp.tm_down)
    return xp, xs, jnp.concatenate([tgp, tup], axis=1), jnp.concatenate([tgs, tus], axis=1)


def _undup_heads(kv_half):
    rows = kv_half.shape[0]
    return kv_half.reshape(rows, N_KV_HEADS, 2, HEAD_DIM)[:, :, 0, :].reshape(rows, KV_COLS)


def kernel(x_prompt, x_sample, state_conv_a, state_pool, cache_win_k, cache_win_v, state_ffn_conv,
           norm_mix, w_in_ab, conv_a, w_pool, pool_scale, w_out_ab,
           w_qkv, b_qkv, sinks, w_o, norm_ffn, w_up, conv_ffn, w_down, norm_final):
    bp = x_prompt.shape[0]
    nb = x_sample.shape[0]
    gp, gs = _prompt_group(bp), _sample_group()
    gp_up = _prompt_group(bp, tm=SEQ)

    def to_tm(a):
        return jnp.swapaxes(a, 0, 1).reshape(-1, a.shape[-1])

    def from_tm(a):
        return jnp.swapaxes(a.reshape(-1, nb, a.shape[-1]), 0, 1)

    def pad_hdr(a, hdr):
        return jnp.concatenate([jnp.zeros((hdr - a.shape[0], a.shape[1]), F32), a], axis=0)

    def last_rows(tail, g, hdr, keep):
        tiles = g.rows // g.tm
        return tail.reshape(tiles, hdr, -1)[g.tps - 1:tiles:g.tps, hdr - keep:]

    def last_steps(tail, keep):
        return from_tm(tail[tail.shape[0] - keep * nb:])

    sink = sinks[0].astype(F32)
    sink_t = jnp.broadcast_to(
        sink.reshape(N_KV_HEADS, GQA_GROUP // 2, 1, 2, 1),
        (N_KV_HEADS, GQA_GROUP // 2, WINDOW, 2, HEAD_DIM)).reshape(N_KV_HEADS, -1, LANES)
    sink_s = jnp.broadcast_to(
        jnp.repeat(sink.reshape(N_KV_HEADS, GQA_GROUP), DEC_SEQ, axis=1)[:, :, None],
        (N_KV_HEADS, GQA_GROUP * DEC_SEQ, LANES))

    xp = x_prompt.reshape(gp.rows, D_MODEL)
    xs = to_tm(x_sample)
    zeros_conv = lambda c: jnp.zeros((bp * gp.hdr_conv, c), F32)

    hp, hs = _norm2(xp, xs, norm_mix[0], gp, gs)
    yap, tail_ap, yas, tail_as = _mix_a(
        hp, hs, w_in_ab[0], conv_a[0], zeros_conv(D_A), to_tm(state_conv_a[0]), gp, gs)
    ybp, tail_pp, ybs, tail_ps = _mix_b(
        hp, hs, w_in_ab[0], w_pool[0], pool_scale[0].reshape(1, D_B),
        jnp.zeros((bp * gp.hdr_pool, D_B), F32), pad_hdr(to_tm(state_pool[0]), gs.hdr_pool), gp, gs)
    xp, xs = _res_mm(xp, xs, [yap, ybp], [yas, ybs], w_out_ab[0], gp, gs, "out_proj")
    xp, xs, tail_f0p, tail_f0s = _ffn(xp, xs, gp, gp_up, gs, 0, norm_ffn[0], w_up, conv_ffn, w_down,
                                      zeros_conv(2 * D_FF), to_tm(state_ffn_conv[0]))

    hp, hs = _norm2(xp, xs, norm_mix[1], gp, gs)
    tabp = _rope_table(gp.tm * gp.tps, gp.pos0, gp.shift)
    tabs = _rope_table(gs.tm, gs.pos0, gs.shift)
    qp, qs, kvp, kvs = _qkv(hp, hs, w_qkv[0], b_qkv[0].reshape(1, -1), tabp, tabs, gp, gs)

    op = _attn_prompt(qp, kvp, sink_t)
    kv_last = kvp.reshape(bp, SEQ, 2 * KV_DUP)[:, SEQ - WINDOW:].reshape(bp * WINDOW, 2 * KV_DUP)
    k_p = _undup_heads(kv_last[:, :KV_DUP]).reshape(bp, WINDOW, N_KV_HEADS, HEAD_DIM)
    v_p = _undup_heads(kv_last[:, KV_DUP:]).reshape(bp, WINDOW, N_KV_HEADS, HEAD_DIM)

    n_keys = WINDOW + DEC_SEQ
    kk = jnp.concatenate([cache_win_k[0].reshape(nb, WINDOW, KV_COLS),
                          from_tm(_undup_heads(kvs[:, :KV_DUP]))], axis=1)
    vv = jnp.concatenate([cache_win_v[0].reshape(nb, WINDOW, KV_COLS),
                          from_tm(_undup_heads(kvs[:, KV_DUP:]))], axis=1)

    def heads_major(a):
        a = jnp.pad(a, ((0, 0), (0, 2 * WINDOW - n_keys), (0, 0)))
        return a.reshape(nb, 2 * WINDOW, N_KV_HEADS, HEAD_DIM).transpose(0, 2, 1, 3)

    q5 = from_tm(qs).reshape(nb, DEC_SEQ, N_KV_HEADS, GQA_GROUP, HEAD_DIM)
    q5 = q5.transpose(0, 2, 3, 1, 4).reshape(nb, N_KV_HEADS, GQA_GROUP * DEC_SEQ, HEAD_DIM)
    o5 = _attn_sample(q5, heads_major(kk), heads_major(vv), sink_s, n_keys)
    o5 = o5.reshape(nb, N_KV_HEADS, GQA_GROUP, DEC_SEQ, HEAD_DIM).transpose(0, 3, 1, 2, 4)
    os_ = to_tm(o5.reshape(nb, DEC_SEQ, Q_COLS))
    k_s = kk[:, n_keys - WINDOW:].reshape(nb, WINDOW, N_KV_HEADS, HEAD_DIM)
    v_s = vv[:, n_keys - WINDOW:].reshape(nb, WINDOW, N_KV_HEADS, HEAD_DIM)

    xp, xs = _res_mm(xp, xs, [op], [os_], w_o[0], gp, gs, "attn_out")
    xp, xs, tail_f1p, tail_f1s = _ffn(xp, xs, gp, gp_up, gs, 1, norm_ffn[1], w_up, conv_ffn, w_down,
                                      zeros_conv(2 * D_FF), to_tm(state_ffn_conv[1]))
    yp, ys = _norm2(xp, xs, norm_final, gp, gs, F32)

    ca_p = last_rows(tail_ap, gp, gp.hdr_conv, 2)[None]
    pl_p = last_rows(tail_pp, gp, gp.hdr_pool, POOL_CTX)[None]
    f_p = jnp.stack([last_rows(t, gp_up, gp_up.hdr_conv, 2) for t in (tail_f0p, tail_f1p)])
    ca_s = last_steps(tail_as, 2)[None]
    pl_s = last_steps(tail_ps, POOL_CTX)[None]
    f_s = jnp.stack([last_steps(t, 2) for t in (tail_f0s, tail_f1s)])
    return (yp.reshape(bp, SEQ, D_MODEL), from_tm(ys), ca_p, ca_s, pl_p, pl_s,
            k_p[None], k_s[None], v_p[None], v_s[None], f_p, f_s)
```
